```python
import jax, jax.numpy as jnp
from jax import lax
import numpy as np

D_MODEL = 1024
BATCH = 2
SEQ = 8192
DEPTH = 4
DEC_BATCH = 128
DEC_SEQ = 8
PAST_LEN = 8192
PAGE_SIZE = 128

V_HEAD = 128
MLA_HEADS = D_MODEL // V_HEAD
Q_LORA = 3 * D_MODEL // 8
KV_LORA = D_MODEL // 4
QK_NOPE = 64
QK_ROPE = 32
RET_HEADS = 4
RET_V = D_MODEL // RET_HEADS
RET_QK = RET_V // 2
RET_CHUNK = 128
D_FF = 4 * D_MODEL
D_IN = Q_LORA + KV_LORA + QK_ROPE + 2 * RET_HEADS * RET_QK + RET_HEADS * RET_V + 3 * D_MODEL
Q_BLOCK = 128
ROPE_THETA = 10000.0
EPS = 1e-6

kernel_name = "mla_retention_gated_hybrid_step"


def _split_points():
    sizes = (Q_LORA, KV_LORA, QK_ROPE, RET_HEADS * RET_QK, RET_HEADS * RET_QK,
             RET_HEADS * RET_V, D_MODEL, D_MODEL, D_MODEL)
    pts, acc = [], 0
    for s in sizes[:-1]:
        acc += s
        pts.append(acc)
    return pts


def rmsnorm(x, g):
    xf = x.astype(jnp.float32)
    y = xf * lax.rsqrt(jnp.mean(xf * xf, axis=-1, keepdims=True) + EPS)
    return (y * g.astype(jnp.float32)).astype(x.dtype)


def rope_cos_sin(pos, dim):
    inv = ROPE_THETA ** (-jnp.arange(0, dim, 2, dtype=jnp.float32) / dim)
    ang = pos.astype(jnp.float32)[:, None] * inv[None, :]
    return jnp.cos(ang), jnp.sin(ang)


def apply_rope(x, cos, sin):
    shape = (1, cos.shape[0]) + (1,) * (x.ndim - 3) + (cos.shape[1],)
    c, s = cos.reshape(shape), sin.reshape(shape)
    xf = x.astype(jnp.float32)
    half = x.shape[-1] // 2
    x1, x2 = xf[..., :half], xf[..., half:]
    return jnp.concatenate([x1 * c - x2 * s, x2 * c + x1 * s], axis=-1).astype(x.dtype)


def mla_expand(c_kv, w_ukv, g_kn):
    kv = (c_kv @ w_ukv).reshape(c_kv.shape[:-1] + (MLA_HEADS, QK_NOPE + V_HEAD))
    return rmsnorm(kv[..., :QK_NOPE], g_kn), kv[..., QK_NOPE:]


def mla_prompt(q_nope, q_rope, c_kv, k_rope, w_ukv, g_kn):
    B, S = q_nope.shape[:2]
    k_nope, v = mla_expand(c_kv, w_ukv, g_kn)
    nb = S // Q_BLOCK
    qn_b = q_nope.reshape(B, nb, Q_BLOCK, MLA_HEADS, QK_NOPE).swapaxes(0, 1)
    qr_b = q_rope.reshape(B, nb, Q_BLOCK, MLA_HEADS, QK_ROPE).swapaxes(0, 1)
    kpos = jnp.arange(S)
    scale = (QK_NOPE + QK_ROPE) ** -0.5

    def block(args):
        i, qn, qr = args
        s = (jnp.einsum('bqhd,bkhd->bhqk', qn, k_nope).astype(jnp.float32)
             + jnp.einsum('bqhr,bkr->bhqk', qr, k_rope).astype(jnp.float32)) * scale
        qpos = i * Q_BLOCK + jnp.arange(Q_BLOCK)
        s = jnp.where(kpos[None, :] <= qpos[:, None], s, -jnp.inf)
        pr = jax.nn.softmax(s, axis=-1).astype(v.dtype)
        return jnp.einsum('bhqk,bkhd->bqhd', pr, v)

    o = lax.map(block, (jnp.arange(nb), qn_b, qr_b))
    return o.swapaxes(0, 1).reshape(B, S, MLA_HEADS * V_HEAD)


def mla_sample(q_nope, q_rope, c_new, kr_new, pool_ckv, pool_kr, page_table, w_ukv, g_kn):
    T = q_nope.shape[1]
    past = page_table.shape[1] * pool_ckv.shape[1]
    kidx = jnp.arange(past + T)
    qidx = past + jnp.arange(T)
    mask = kidx[None, :] <= qidx[:, None]
    scale = (QK_NOPE + QK_ROPE) ** -0.5

    def one(args):
        pids, qn, qr, cn, krn = args
        c = jnp.concatenate([pool_ckv[pids].reshape(past, KV_LORA).astype(cn.dtype), cn], axis=0)
        kr = jnp.concatenate([pool_kr[pids].reshape(past, QK_ROPE).astype(krn.dtype), krn], axis=0)
        k_nope, v = mla_expand(c, w_ukv, g_kn)
        s = (jnp.einsum('qhd,khd->hqk', qn, k_nope).astype(jnp.float32)
             + jnp.einsum('qhr,kr->hqk', qr, kr).astype(jnp.float32)) * scale
        s = jnp.where(mask[None], s, -jnp.inf)
        pr = jax.nn.softmax(s, axis=-1).astype(v.dtype)
        return jnp.einsum('hqk,khd->qhd', pr, v).reshape(T, MLA_HEADS * V_HEAD)

    return lax.map(one, (page_table, q_nope, q_rope, c_new, kr_new))


def ret_log_gamma():
    return jnp.log1p(-jnp.exp2(-5.0 - jnp.arange(RET_HEADS, dtype=jnp.float32)))


def retention_chunk(S, q, k, v):
    T = q.shape[1]
    lg = ret_log_gamma()
    t = jnp.arange(T, dtype=jnp.float32)
    diff = t[:, None] - t[None, :]
    decay = jnp.where(diff >= 0, jnp.exp(lg[:, None, None] * jnp.maximum(diff, 0.0)), 0.0)
    qf, kf, vf = q.astype(jnp.float32), k.astype(jnp.float32), v.astype(jnp.float32)
    a = jnp.einsum('bihd,bjhd->bhij', qf, kf) * decay[None]
    o = jnp.einsum('bhij,bjhe->bihe', a, vf)
    cross_decay = jnp.exp(lg[None, :] * (t[:, None] + 1.0))
    o = o + jnp.einsum('bihd,bhde->bihe', qf, S) * cross_decay[None, :, :, None]
    end_decay = jnp.exp(lg[None, :] * (T - 1.0 - t[:, None]))
    S_new = (jnp.exp(lg * T)[None, :, None, None] * S
             + jnp.einsum('bjhd,bjhe,jh->bhde', kf, vf, end_decay))
    return S_new, o


def retention_prompt(q, k, v):
    B, S = q.shape[:2]
    nc = S // RET_CHUNK

    def rs(a):
        return a.reshape((B, nc, RET_CHUNK) + a.shape[2:]).swapaxes(0, 1)

    S0 = jnp.zeros((B, RET_HEADS, RET_QK, RET_V), jnp.float32)
    S_fin, o = lax.scan(lambda st, xs: retention_chunk(st, *xs), S0, (rs(q), rs(k), rs(v)))
    return o.swapaxes(0, 1).reshape(B, S, RET_HEADS, RET_V), S_fin


def retention_sample(S_prev, q, k, v):
    S_new, o = retention_chunk(S_prev.astype(jnp.float32), q, k, v)
    return o, S_new


def head_layernorm(o, g):
    mu = jnp.mean(o, axis=-1, keepdims=True)
    var = jnp.mean(jnp.square(o - mu), axis=-1, keepdims=True)
    y = ((o - mu) * lax.rsqrt(var + EPS)).reshape(o.shape[:2] + (RET_HEADS * RET_V,))
    return y * g.astype(jnp.float32)


def trunk_layer(x, pos, p, mla_core, ret_core):
    B, T = x.shape[:2]
    h = rmsnorm(x, p['ln1'])
    z = h @ p['w_in']
    q_lat, kv_lat, k_r, r_q, r_k, r_v, r_g, g_a, g_b = jnp.split(z, _split_points(), axis=-1)
    cos_m, sin_m = rope_cos_sin(pos, QK_ROPE)
    cos_r, sin_r = rope_cos_sin(pos, RET_QK)
    q = (rmsnorm(q_lat, p['g_qlat']) @ p['w_uq']).reshape(B, T, MLA_HEADS, QK_NOPE + QK_ROPE)
    q_nope = rmsnorm(q[..., :QK_NOPE], p['g_qn'])
    q_rope = apply_rope(rmsnorm(q[..., QK_NOPE:], p['g_qr']), cos_m, sin_m)
    c_kv = rmsnorm(kv_lat, p['g_kvlat'])
    k_rope = apply_rope(rmsnorm(k_r, p['g_kr']), cos_m, sin_m)
    o_mla = mla_core(q_nope, q_rope, c_kv, k_rope)
    rq = apply_rope(r_q.reshape(B, T, RET_HEADS, RET_QK), cos_r, sin_r)
    rk = apply_rope(r_k.reshape(B, T, RET_HEADS, RET_QK), cos_r, sin_r) * (RET_QK ** -0.5)
    rv = r_v.reshape(B, T, RET_HEADS, RET_V)
    o_ret, s_new = ret_core(rq, rk, rv)
    o_ret = head_layernorm(o_ret, p['g_ret']).astype(x.dtype) * jax.nn.silu(r_g)
    mixed = jax.nn.sigmoid(g_a) * o_mla + jax.nn.sigmoid(g_b) * o_ret
    x = x + mixed @ p['w_o']
    h2 = rmsnorm(x, p['ln2'])
    x = x + jnp.square(jax.nn.relu(h2 @ p['w_up'])) @ p['w_down']
    return x, c_kv, k_rope, s_new.astype(x.dtype)


def setup_inputs(seed: int = 0) -> dict:
    key = jax.random.key(seed)
    ks = jax.random.split(key, 24)
    n_pages = PAST_LEN // PAGE_SIZE
    n_used = DEC_BATCH * n_pages
    n_pool = n_used + n_used // 4
    nrm = jax.random.normal

    def w(k, shape, fan_in):
        return nrm(k, shape, jnp.float32) * (fan_in ** -0.5)

    def gain(k, shape):
        return 1.0 + 0.02 * nrm(k, shape, jnp.float32)

    return {
        'x_prompt': nrm(ks[0], (BATCH, SEQ, D_MODEL), jnp.float32),
        'x_sample': nrm(ks[1], (DEC_BATCH, DEC_SEQ, D_MODEL), jnp.float32),
        'cache_ckv': nrm(ks[2], (DEPTH, n_pool, PAGE_SIZE, KV_LORA), jnp.float32),
        'cache_krope': nrm(ks[3], (DEPTH, n_pool, PAGE_SIZE, QK_ROPE), jnp.float32),
        'state_ret': w(ks[4], (DEPTH, DEC_BATCH, RET_HEADS, RET_QK, RET_V), RET_QK),
        'page_table': jax.random.permutation(ks[5], n_pool)[:n_used].reshape(DEC_BATCH, n_pages).astype(jnp.int32),
        'ln1': gain(ks[6], (DEPTH, D_MODEL)),
        'w_in': w(ks[7], (DEPTH, D_MODEL, D_IN), D_MODEL),
        'g_qlat': gain(ks[8], (DEPTH, Q_LORA)),
        'w_uq': w(ks[9], (DEPTH, Q_LORA, MLA_HEADS * (QK_NOPE + QK_ROPE)), Q_LORA),
        'g_qn': gain(ks[10], (DEPTH, QK_NOPE)),
        'g_qr': gain(ks[11], (DEPTH, QK_ROPE)),
        'g_kvlat': gain(ks[12], (DEPTH, KV_LORA)),
        'g_kr': gain(ks[13], (DEPTH, QK_ROPE)),
        'w_ukv': w(ks[14], (DEPTH, KV_LORA, MLA_HEADS * (QK_NOPE + V_HEAD)), KV_LORA),
        'g_kn': gain(ks[15], (DEPTH, QK_NOPE)),
        'g_ret': gain(ks[16], (DEPTH, RET_HEADS * RET_V)),
        'w_o': w(ks[17], (DEPTH, D_MODEL, D_MODEL), D_MODEL),
        'ln2': gain(ks[18], (DEPTH, D_MODEL)),
        'w_up': w(ks[19], (DEPTH, D_MODEL, D_FF), D_MODEL),
        'w_down': w(ks[20], (DEPTH, D_FF, D_MODEL), D_FF),
    }


def reference(x_prompt, x_sample, cache_ckv, cache_krope, state_ret, page_table,
              ln1, w_in, g_qlat, w_uq, g_qn, g_qr, g_kvlat, g_kr, w_ukv, g_kn, g_ret,
              w_o, ln2, w_up, w_down):
    S = x_prompt.shape[1]
    T = x_sample.shape[1]
    past = page_table.shape[1] * cache_ckv.shape[2]
    pos_p = jnp.arange(S)
    pos_s = past + jnp.arange(T)
    yp, ys = x_prompt, x_sample
    ckv_p, kr_p, st_p, ckv_s, kr_s, st_s = [], [], [], [], [], []
    for l in range(DEPTH):
        p = {'ln1': ln1[l], 'w_in': w_in[l], 'g_qlat': g_qlat[l], 'w_uq': w_uq[l],
             'g_qn': g_qn[l], 'g_qr': g_qr[l], 'g_kvlat': g_kvlat[l], 'g_kr': g_kr[l],
             'g_ret': g_ret[l], 'w_o': w_o[l], 'ln2': ln2[l], 'w_up': w_up[l], 'w_down': w_down[l]}
        wl_ukv, gl_kn = w_ukv[l], g_kn[l]
        pool_c, pool_k, st_prev = cache_ckv[l], cache_krope[l], state_ret[l]

        def core_p(qn, qr, c, kr):
            return mla_prompt(qn, qr, c, kr, wl_ukv, gl_kn)

        def core_s(qn, qr, c, kr):
            return mla_sample(qn, qr, c, kr, pool_c, pool_k, page_table, wl_ukv, gl_kn)

        def ret_s(q, k, v):
            return retention_sample(st_prev, q, k, v)

        yp, c1, k1, s1 = trunk_layer(yp, pos_p, p, core_p, retention_prompt)
        ys, c2, k2, s2 = trunk_layer(ys, pos_s, p, core_s, ret_s)
        ckv_p.append(c1); kr_p.append(k1); st_p.append(s1)
        ckv_s.append(c2); kr_s.append(k2); st_s.append(s2)
    return (yp, ys, jnp.stack(ckv_p), jnp.stack(kr_p), jnp.stack(st_p),
            jnp.stack(ckv_s), jnp.stack(kr_s), jnp.stack(st_s))
```

```python
import functools

import numpy as np
import jax
import jax.numpy as jnp
from jax import lax
from jax.experimental import pallas as pl
from jax.experimental.pallas import tpu as pltpu

F32 = jnp.float32
BF16 = jnp.bfloat16

D_MODEL = 1024
MLA_HEADS = 8
V_HEAD = 128
Q_LORA = 384
KV_LORA = 256
QK_NOPE = 64
QK_ROPE = 32
RET_HEADS = 4
RET_V = 256
RET_QK = 128
D_FF = 4096
ROPE_THETA = 10000.0
EPS = 1e-6

LANES = 128
VMEM_LIMIT = 52 << 20

_SEG = {}
_off = 0
for _name, _w in (("qlat", Q_LORA), ("kvlat", KV_LORA), ("rq", RET_HEADS * RET_QK),
                  ("rk", RET_HEADS * RET_QK), ("rv", RET_HEADS * RET_V), ("rg", D_MODEL),
                  ("ga", D_MODEL), ("gb", D_MODEL), ("kr", LANES)):
    _SEG[_name] = (_off, _off + _w)
    _off += _w
D_IN_P = _off
ROPE_LO = QK_NOPE
ROPE_HALF = QK_ROPE // 2


def _group_mean_matrix():
    gid = np.array([0] * QK_NOPE + [1] * QK_ROPE + [2] * (LANES - QK_NOPE - QK_ROPE))
    size = np.array([QK_NOPE, QK_ROPE, LANES - QK_NOPE - QK_ROPE], np.float32)
    g = (gid[:, None] == gid[None, :]).astype(np.float32) / size[gid][None, :]
    out = np.zeros((2 * LANES, 2 * LANES), np.float32)
    out[:LANES, :LANES] = g
    out[LANES:, LANES:] = g
    return out


def _const_spec(shape):
    nd = len(shape)
    return pl.BlockSpec(shape, lambda *_: (0,) * nd, pipeline_mode=pl.Buffered(1))


def _rmsnorm(x, g):
    return x * lax.rsqrt(jnp.mean(x * x, axis=-1, keepdims=True) + EPS) * g


def _sigmoid(x):
    return 1.0 / (1.0 + jnp.exp(-x))


def _dot(a, b):
    return jnp.dot(a, b, preferred_element_type=F32)


def _dot_nt(a, b):
    return lax.dot_general(a, b, (((1,), (1,)), ((), ())), preferred_element_type=F32)


def _head_norm(v, g_ref, gain2):
    heads = []
    for p in range(MLA_HEADS // 2):
        vp = v[:, p * 2 * LANES:(p + 1) * 2 * LANES]
        ms = _dot((vp * vp).astype(BF16), g_ref[...])
        n = vp * lax.rsqrt(ms + EPS) * gain2
        heads.append(n[:, :LANES])
        heads.append(n[:, LANES:])
    return heads


def _rope32(n, cm, sm):
    lane = lax.broadcasted_iota(jnp.int32, n.shape, 1)
    rot = jnp.where(lane < ROPE_LO + ROPE_HALF,
                    pltpu.roll(n, LANES - ROPE_HALF, axis=1),
                    pltpu.roll(n, ROPE_HALF, axis=1))
    return n * cm + rot * sm


def _in_kernel(mode, x_ref, ln1_ref, win_ref, gql_ref, wuq_ref, gq_ref, gkvl_ref, gkr_ref,
               g_ref, cr_ref, sr_ref, cm_ref, sm_ref, wk_ref, gk_ref, *rest):
    if mode == "prompt":
        wuvt_ref = rest[0]
        q_o, ckv_o, kr_o, rq_o, rk_o, rv_o, rg_o, ga_o, gb_o, k_o, vt_o = rest[1:]
    else:
        q_o, ckv_o, kr_o, rq_o, rk_o, rv_o, rg_o, ga_o, gb_o, qabs_o = rest

    h = _rmsnorm(x_ref[...], ln1_ref[...]).astype(BF16)

    def seg(name):
        a, b = _SEG[name]
        return _dot(h, win_ref[:, a:b])

    cm = cm_ref[...]
    sm = sm_ref[...]

    qn = _rmsnorm(seg("qlat"), gql_ref[...]).astype(BF16)
    q = _dot(qn, wuq_ref[...])
    gq2 = jnp.concatenate([gq_ref[...], gq_ref[...]], axis=1)
    q_heads = [_rope32(n, cm, sm) for n in _head_norm(q, g_ref, gq2)]
    for hh in range(MLA_HEADS):
        q_o[:, hh * LANES:(hh + 1) * LANES] = q_heads[hh].astype(q_o.dtype)

    c = _rmsnorm(seg("kvlat"), gkvl_ref[...])
    ckv_o[...] = c
    cb = c.astype(BF16)
    kr = seg("kr")
    ms = _dot((kr * kr).astype(BF16), g_ref[:LANES, :LANES])
    krr = _rope32(kr * lax.rsqrt(ms + EPS) * gkr_ref[...], cm, sm)
    kr_o[...] = krr[:, ROPE_LO:ROPE_LO + QK_ROPE]

    if mode == "prompt":
        gk2 = jnp.concatenate([gk_ref[...], gk_ref[...]], axis=1)
        k_heads = _head_norm(_dot(cb, wk_ref[...]), g_ref, gk2)
        for hh in range(MLA_HEADS):
            k_o[:, hh * LANES:(hh + 1) * LANES] = (k_heads[hh] + krr).astype(k_o.dtype)
        vt_o[...] = _dot_nt(wuvt_ref[...], cb).astype(vt_o.dtype)
    else:
        gk = gk_ref[...]
        for hh in range(MLA_HEADS):
            qg = (q_heads[hh] * gk).astype(BF16)
            qabs_o[:, hh * KV_LORA:(hh + 1) * KV_LORA] = _dot(
                qg, wk_ref[hh * LANES:(hh + 1) * LANES, :]).astype(qabs_o.dtype)

    cr = cr_ref[...]
    sr = sr_ref[...]
    rq = seg("rq")
    rk = seg("rk")
    for hh in range(RET_HEADS):
        sl = slice(hh * RET_QK, (hh + 1) * RET_QK)
        a = rq[:, sl]
        rq_o[:, sl] = (a * cr + pltpu.roll(a, RET_QK // 2, axis=1) * sr).astype(rq_o.dtype)
        b = rk[:, sl]
        rk_o[:, sl] = ((b * cr + pltpu.roll(b, RET_QK // 2, axis=1) * sr)
                       * (RET_QK ** -0.5)).astype(rk_o.dtype)
    rv_o[...] = seg("rv").astype(rv_o.dtype)
    rg_o[...] = seg("rg")
    ga_o[...] = seg("ga")
    gb_o[...] = seg("gb")


def _in_proj(mode, x, w, tabs, n_seq_blocks, tm, batch, seq):
    T = x.shape[0]
    nt = T // tm
    act = BF16 if mode == "prompt" else F32
    row = lambda width: pl.BlockSpec((tm, width), lambda i: (i, 0))
    tab = pl.BlockSpec((tm, LANES), lambda i: (i % n_seq_blocks, 0))
    in_specs = [row(D_MODEL), _const_spec((1, D_MODEL)), _const_spec((D_MODEL, D_IN_P)),
                _const_spec((1, Q_LORA)), _const_spec((Q_LORA, MLA_HEADS * LANES)),
                _const_spec((1, LANES)), _const_spec((1, KV_LORA)), _const_spec((1, LANES)),
                _const_spec((2 * LANES, 2 * LANES)), tab, tab, tab, tab]
    args = [x, w["ln1"], w["w_in"], w["g_qlat"], w["w_uq"], w["g_q"], w["g_kvlat"], w["g_kr"],
            w["gmat"], tabs["cr"], tabs["sr"], tabs["cm"], tabs["sm"]]
    out_shape = [jax.ShapeDtypeStruct((T, MLA_HEADS * LANES), act),
                 jax.ShapeDtypeStruct((T, KV_LORA), F32),
                 jax.ShapeDtypeStruct((T, QK_ROPE), F32),
                 jax.ShapeDtypeStruct((T, RET_HEADS * RET_QK), act),
                 jax.ShapeDtypeStruct((T, RET_HEADS * RET_QK), act),
                 jax.ShapeDtypeStruct((T, RET_HEADS * RET_V), act),
                 jax.ShapeDtypeStruct((T, D_MODEL), F32),
                 jax.ShapeDtypeStruct((T, D_MODEL), F32),
                 jax.ShapeDtypeStruct((T, D_MODEL), F32)]
    out_specs = [row(MLA_HEADS * LANES), row(KV_LORA), row(QK_ROPE), row(RET_HEADS * RET_QK),
                 row(RET_HEADS * RET_QK), row(RET_HEADS * RET_V), row(D_MODEL), row(D_MODEL),
                 row(D_MODEL)]
    if mode == "prompt":
        in_specs += [_const_spec((KV_LORA, MLA_HEADS * LANES)), _const_spec((1, LANES)),
                     _const_spec((MLA_HEADS * V_HEAD, KV_LORA))]
        args += [w["w_uk"], w["g_k"], w["w_uvt"]]
        out_shape += [jax.ShapeDtypeStruct((T, MLA_HEADS * LANES), BF16),
                      jax.ShapeDtypeStruct((batch, MLA_HEADS * V_HEAD, seq), BF16)]
        out_specs += [row(MLA_HEADS * LANES),
                      pl.BlockSpec((None, MLA_HEADS * V_HEAD, tm),
                                   lambda i: (i // n_seq_blocks, 0, i % n_seq_blocks))]
    else:
        in_specs += [_const_spec((MLA_HEADS * LANES, KV_LORA)), _const_spec((1, LANES))]
        args += [w["w_ukt"], w["g_k"]]
        out_shape += [jax.ShapeDtypeStruct((T, MLA_HEADS * KV_LORA), F32)]
        out_specs += [row(MLA_HEADS * KV_LORA)]
    return pl.pallas_call(
        functools.partial(_in_kernel, mode),
        grid=(nt,), in_specs=in_specs, out_specs=out_specs, out_shape=out_shape,
        compiler_params=pltpu.CompilerParams(dimension_semantics=("parallel",),
                                             vmem_limit_bytes=VMEM_LIMIT),
        name="in_proj_" + mode,
    )(*args)


def _flash_kernel(tq, tk, q_ref, k_ref, vt_ref, o_ref, m_ref, l_ref, acc_ref):
    qi = pl.program_id(2)
    q = q_ref[...]
    m_ref[...] = jnp.full(m_ref.shape, -jnp.inf, F32)
    l_ref[...] = jnp.zeros(l_ref.shape, F32)
    acc_ref[...] = jnp.zeros(acc_ref.shape, F32)

    def block(ki, masked):
        k0 = pl.multiple_of(ki * tk, tk)
        s = _dot_nt(k_ref[pl.ds(k0, tk), :], q)
        if masked:
            kpos = k0 + lax.broadcasted_iota(jnp.int32, (tk, tq), 0)
            qpos = qi * tq + lax.broadcasted_iota(jnp.int32, (tk, tq), 1)
            s = jnp.where(kpos <= qpos, s, -jnp.inf)
        m_prev = m_ref[...]
        m_new = jnp.maximum(m_prev, jnp.max(s, axis=0, keepdims=True))
        alpha = jnp.exp(m_prev - m_new)
        p = jnp.exp(s - m_new)
        l_ref[...] = alpha * l_ref[...] + jnp.sum(p, axis=0, keepdims=True)
        acc_ref[...] = alpha * acc_ref[...] + _dot(vt_ref[:, pl.ds(k0, tk)], p.astype(BF16))
        m_ref[...] = m_new

    n_full = (qi * tq) // tk

    def body(ki, carry):
        block(ki, False)
        return carry

    lax.fori_loop(0, n_full, body, 0)
    for d in range(tq // tk):
        block(n_full + d, True)
    o_ref[...] = (acc_ref[...] / l_ref[...]).T.astype(o_ref.dtype)


def _mla_prompt(q, k, vt, batch, seq, tq, tk):
    nq = seq // tq
    return pl.pallas_call(
        functools.partial(_flash_kernel, tq, tk),
        grid=(batch, MLA_HEADS, nq),
        in_specs=[pl.BlockSpec((tq, LANES), lambda b, h, i: (b * nq + i, h)),
                  pl.BlockSpec((seq, LANES), lambda b, h, i: (b, h)),
                  pl.BlockSpec((None, V_HEAD, seq), lambda b, h, i: (b, h, 0))],
        out_specs=pl.BlockSpec((tq, V_HEAD), lambda b, h, i: (b * nq + i, h)),
        out_shape=jax.ShapeDtypeStruct((batch * seq, MLA_HEADS * V_HEAD), BF16),
        scratch_shapes=[pltpu.VMEM((1, tq), F32), pltpu.VMEM((1, tq), F32),
                        pltpu.VMEM((V_HEAD, tq), F32)],
        compiler_params=pltpu.CompilerParams(
            dimension_semantics=("parallel", "parallel", "arbitrary"),
            vmem_limit_bytes=VMEM_LIMIT),
        name="mla_prompt",
    )(q, k, vt)


def _group_norm(o):
    mu = jnp.mean(o, axis=-1, keepdims=True)
    d = o - mu
    return d * lax.rsqrt(jnp.mean(d * d, axis=-1, keepdims=True) + EPS)


def _decay_tables(lg, c):
    ti = lax.broadcasted_iota(jnp.int32, (c, c), 0)
    tj = lax.broadcasted_iota(jnp.int32, (c, c), 1)
    diff = (ti - tj).astype(F32)
    decay = jnp.where(diff >= 0, jnp.exp(lg * jnp.maximum(diff, 0.0)), 0.0)
    t = lax.broadcasted_iota(jnp.int32, (c, 1), 0).astype(F32)
    cross = jnp.exp(lg * (t + 1.0))
    end = jnp.exp(lg * (c - 1.0 - t))
    total = jnp.exp(lg * float(c))
    return decay, cross, end, total


def _ret_prompt_kernel(chunk, n_chunks, lg_ref, rq_ref, rk_ref, rv_ref, y_ref, st_ref, s_ref):
    lg = lg_ref[0][0:1, 0:1]
    decay, cross, end, total = _decay_tables(lg, chunk)
    s_ref[...] = jnp.zeros(s_ref.shape, F32)

    def body(ci, carry):
        r0 = pl.multiple_of(ci * chunk, chunk)
        q = rq_ref[pl.ds(r0, chunk), :]
        k = rk_ref[pl.ds(r0, chunk), :]
        v = rv_ref[pl.ds(r0, chunk), :]
        state = s_ref[...]
        a = _dot_nt(q, k) * decay
        o = _dot(a.astype(BF16), v) + _dot(q, state.astype(BF16)) * cross
        kd = (k.astype(F32) * end).astype(BF16)
        s_ref[...] = total * state + lax.dot_general(
            kd, v, (((0,), (0,)), ((), ())), preferred_element_type=F32)
        y_ref[pl.ds(r0, chunk), :] = _group_norm(o)
        return carry

    lax.fori_loop(0, n_chunks, body, 0)
    st_ref[...] = s_ref[...]


def _ret_prompt(lgtab, rq, rk, rv, batch, seq, chunk):
    return pl.pallas_call(
        functools.partial(_ret_prompt_kernel, chunk, seq // chunk),
        grid=(batch, RET_HEADS),
        in_specs=[pl.BlockSpec((1, 8, LANES), lambda b, h: (h, 0, 0)),
                  pl.BlockSpec((seq, RET_QK), lambda b, h: (b, h)),
                  pl.BlockSpec((seq, RET_QK), lambda b, h: (b, h)),
                  pl.BlockSpec((seq, RET_V), lambda b, h: (b, h))],
        out_specs=[pl.BlockSpec((seq, RET_V), lambda b, h: (b, h)),
                   pl.BlockSpec((None, None, RET_QK, RET_V), lambda b, h: (b, h, 0, 0))],
        out_shape=[jax.ShapeDtypeStruct((batch * seq, RET_HEADS * RET_V), F32),
                   jax.ShapeDtypeStruct((batch, RET_HEADS, RET_QK, RET_V), F32)],
        scratch_shapes=[pltpu.VMEM((RET_QK, RET_V), F32)],
        compiler_params=pltpu.CompilerParams(dimension_semantics=("parallel", "parallel"),
                                             vmem_limit_bytes=VMEM_LIMIT),
        name="ret_prompt",
    )(lgtab, rq, rk, rv)


def _ret_sample_kernel(bb, t, lg_ref, rq_ref, rk_ref, rv_ref, st_ref, y_ref, stn_ref):
    for h in range(RET_HEADS):
        lg = lg_ref[h][0:1, 0:1]
        decay, cross, end, total = _decay_tables(lg, t)
        for bi in range(bb):
            rows = slice(bi * t, (bi + 1) * t)
            q = rq_ref[rows, h * RET_QK:(h + 1) * RET_QK].astype(BF16)
            k = rk_ref[rows, h * RET_QK:(h + 1) * RET_QK]
            v = rv_ref[rows, h * RET_V:(h + 1) * RET_V].astype(BF16).astype(F32)
            state = st_ref[bi, h]
            a = (_dot_nt(q, k.astype(BF16)) * decay).astype(BF16).astype(F32)
            o = _dot(q, state.astype(BF16)) * cross
            for j in range(t):
                o = o + a[:, j:j + 1] * v[j:j + 1, :]
            y_ref[rows, h * RET_V:(h + 1) * RET_V] = _group_norm(o)
            kd = (k * end).astype(BF16).astype(F32)
            stn_ref[bi, h] = total * state + _dot(kd.T, v)


def _ret_sample(layer, lgtab, rq, rk, rv, state, bb):
    dec_batch = state.shape[1]
    t = rq.shape[0] // dec_batch
    row = lambda width: pl.BlockSpec((bb * t, width), lambda i: (i, 0))
    st_blk = (bb, RET_HEADS, RET_QK, RET_V)
    return pl.pallas_call(
        functools.partial(_ret_sample_kernel, bb, t),
        grid=(dec_batch // bb,),
        in_specs=[_const_spec((RET_HEADS, 8, LANES)), row(RET_HEADS * RET_QK),
                  row(RET_HEADS * RET_QK), row(RET_HEADS * RET_V),
                  pl.BlockSpec((None,) + st_blk, lambda i: (layer, i, 0, 0, 0))],
        out_specs=[row(RET_HEADS * RET_V), pl.BlockSpec(st_blk, lambda i: (i, 0, 0, 0))],
        out_shape=[jax.ShapeDtypeStruct((dec_batch * t, RET_HEADS * RET_V), F32),
                   jax.ShapeDtypeStruct(state.shape[1:], F32)],
        compiler_params=pltpu.CompilerParams(dimension_semantics=("parallel",),
                                             vmem_limit_bytes=VMEM_LIMIT),
        name="ret_sample",
    )(lgtab, rq, rk, rv, state)


def _sattn_kernel(layer, n_pages, chunk_pages, t, pt_ref, q_ref, qabs_ref, cnew_ref, krnew_ref,
                  wkt_ref, wv_ref, cc_hbm, ckr_hbm, o_ref, cbuf, krbuf, sem):
    b = pl.program_id(0)
    nb = pl.num_programs(0)
    slot = lax.rem(b, 2)
    page = cbuf.shape[2]
    hq = MLA_HEADS * t

    def copies(bb, sl, p):
        pg = pt_ref[bb, p]
        return (pltpu.make_async_copy(cc_hbm.at[layer, pg], cbuf.at[sl, p], sem.at[0, sl]),
                pltpu.make_async_copy(ckr_hbm.at[layer, pg], krbuf.at[sl, p], sem.at[1, sl]))

    def start_fetch(bb, sl):
        def body(p, carry):
            for cp in copies(bb, sl, p):
                cp.start()
            return carry
        lax.fori_loop(0, n_pages, body, 0)

    def wait_fetch(bb, sl):
        def body(p, carry):
            for cp in copies(bb, sl, p):
                cp.wait()
            return carry
        lax.fori_loop(0, n_pages, body, 0)

    @pl.when(b == 0)
    def _():
        start_fetch(0, 0)

    @pl.when(b + 1 < nb)
    def _():
        start_fetch(b + 1, 1 - slot)

    wait_fetch(b, slot)

    qh = q_ref[...]
    qr = jnp.concatenate(
        [qh[:, h * LANES + ROPE_LO:h * LANES + ROPE_LO + QK_ROPE] for h in range(MLA_HEADS)],
        axis=0).astype(BF16)
    qa = jnp.concatenate(
        [qabs_ref[:, h * KV_LORA:(h + 1) * KV_LORA] for h in range(MLA_HEADS)],
        axis=0).astype(BF16)
    lhs = jnp.concatenate([wkt_ref[...], qa], axis=0)
    n_exp = MLA_HEADS * QK_NOPE

    def scores(c, kr, mask):
        n = c.shape[0]
        cb = c.astype(BF16)
        big = _dot_nt(lhs, cb)
        kt = big[:n_exp].reshape(MLA_HEADS, QK_NOPE, n)
        inv = lax.rsqrt(jnp.mean(kt * kt, axis=1, keepdims=True) + EPS)
        s = (big[n_exp:].reshape(MLA_HEADS, t, n) * inv
             + _dot_nt(qr, kr.astype(BF16)).reshape(MLA_HEADS, t, n))
        if mask is not None:
            s = jnp.where(mask, s, -jnp.inf)
        return s, cb

    def update(carry, s, cb):
        m, l, acc = carry
        n = s.shape[-1]
        m_new = jnp.maximum(m, jnp.max(s, axis=-1, keepdims=True))
        alpha = jnp.exp(m - m_new)
        p = jnp.exp(s - m_new)
        l = alpha * l + jnp.sum(p, axis=-1, keepdims=True)
        acc = alpha.reshape(hq, 1) * acc + _dot(p.reshape(hq, n).astype(BF16), cb)
        return m_new, l, acc

    carry = (jnp.full((MLA_HEADS, t, 1), -jnp.inf, F32), jnp.zeros((MLA_HEADS, t, 1), F32),
             jnp.zeros((hq, KV_LORA), F32))
    n_keys = chunk_pages * page
    for j in range(n_pages // chunk_pages):
        c = cbuf[slot, pl.ds(j * chunk_pages, chunk_pages)].reshape(n_keys, KV_LORA)
        kr = krbuf[slot, pl.ds(j * chunk_pages, chunk_pages)].reshape(n_keys, QK_ROPE)
        s, cb = scores(c, kr, None)
        carry = update(carry, s, cb)

    pad = LANES - t
    c = jnp.concatenate([cnew_ref[...], jnp.zeros((pad, KV_LORA), F32)], axis=0)
    kr = jnp.concatenate([krnew_ref[...], jnp.zeros((pad, QK_ROPE), F32)], axis=0)
    shape = (MLA_HEADS, t, LANES)
    mask = lax.broadcasted_iota(jnp.int32, shape, 2) <= lax.broadcasted_iota(jnp.int32, shape, 1)
    s, cb = scores(c, kr, mask)
    m, l, acc = update(carry, s, cb)

    oc = (acc / l.reshape(hq, 1)).astype(BF16)
    full = _dot(oc, wv_ref[...])
    for h in range(MLA_HEADS):
        o_ref[:, h * V_HEAD:(h + 1) * V_HEAD] = full[h * t:(h + 1) * t, h * V_HEAD:(h + 1) * V_HEAD]


def _mla_sample(layer, page_table, q, qabs, cnew, krnew, wkt, wv, cache_ckv, cache_krope,
                chunk_pages):
    dec_batch, n_pages = page_table.shape
    t = q.shape[0] // dec_batch
    page = cache_ckv.shape[2]
    row = lambda width: pl.BlockSpec((t, width), lambda b, pt: (b, 0))
    const = lambda shape: pl.BlockSpec(shape, lambda b, pt: (0,) * len(shape),
                                       pipeline_mode=pl.Buffered(1))
    grid_spec = pltpu.PrefetchScalarGridSpec(
        num_scalar_prefetch=1, grid=(dec_batch,),
        in_specs=[row(MLA_HEADS * LANES), row(MLA_HEADS * KV_LORA), row(KV_LORA), row(QK_ROPE),
                  const((MLA_HEADS * QK_NOPE, KV_LORA)), const((KV_LORA, MLA_HEADS * V_HEAD)),
                  pl.BlockSpec(memory_space=pl.ANY), pl.BlockSpec(memory_space=pl.ANY)],
        out_specs=row(MLA_HEADS * V_HEAD),
        scratch_shapes=[pltpu.VMEM((2, n_pages, page, KV_LORA), F32),
                        pltpu.VMEM((2, n_pages, page, QK_ROPE), F32),
                        pltpu.SemaphoreType.DMA((2, 2))])
    return pl.pallas_call(
        functools.partial(_sattn_kernel, layer, n_pages, chunk_pages, t),
        grid_spec=grid_spec,
        out_shape=jax.ShapeDtypeStruct((dec_batch * t, MLA_HEADS * V_HEAD), F32),
        compiler_params=pltpu.CompilerParams(dimension_semantics=("arbitrary",),
                                             vmem_limit_bytes=VMEM_LIMIT),
        name="mla_sample",
    )(page_table, q, qabs, cnew, krnew, wkt, wv, cache_ckv, cache_krope)


def _out_kernel(x_ref, om_ref, y_ref, rg_ref, ga_ref, gb_ref, gret_ref, wo_ref, ln2_ref,
                wup_ref, wdn_ref, o_ref):
    rg = rg_ref[...]
    o_ret = y_ref[...] * gret_ref[...] * (rg * _sigmoid(rg))
    mixed = _sigmoid(ga_ref[...]) * om_ref[...].astype(F32) + _sigmoid(gb_ref[...]) * o_ret
    x1 = x_ref[...] + _dot(mixed.astype(BF16), wo_ref[...])
    h2 = _rmsnorm(x1, ln2_ref[...]).astype(BF16)
    u = jnp.maximum(_dot(h2, wup_ref[...]), 0.0)
    o_ref[...] = x1 + _dot((u * u).astype(BF16), wdn_ref[...])


def _out_proj(x, om, y, rg, ga, gb, w, tm):
    T = x.shape[0]
    row = pl.BlockSpec((tm, D_MODEL), lambda i: (i, 0))
    return pl.pallas_call(
        _out_kernel,
        grid=(T // tm,),
        in_specs=[row, row, row, row, row, row, _const_spec((1, D_MODEL)),
                  _const_spec((D_MODEL, D_MODEL)), _const_spec((1, D_MODEL)),
                  _const_spec((D_MODEL, D_FF)), _const_spec((D_FF, D_MODEL))],
        out_specs=row,
        out_shape=jax.ShapeDtypeStruct((T, D_MODEL), F32),
        compiler_params=pltpu.CompilerParams(dimension_semantics=("parallel",),
                                             vmem_limit_bytes=VMEM_LIMIT),
        name="out_proj",
    )(x, om, y, rg, ga, gb, w["g_ret"], w["w_o"], w["ln2"], w["w_up"], w["w_down"])


def _prep_layer(l, ln1, w_in, g_qlat, w_uq, g_qn, g_qr, g_kvlat, g_kr, w_ukv, g_kn, g_ret,
                w_o, ln2, w_up, w_down):
    wi = w_in[l]
    lat = Q_LORA + KV_LORA
    zeros = lambda n: jnp.zeros((D_MODEL, n), F32)
    w_in_p = jnp.concatenate(
        [wi[:, :lat], wi[:, lat + QK_ROPE:], zeros(ROPE_LO), wi[:, lat:lat + QK_ROPE],
         zeros(LANES - ROPE_LO - QK_ROPE)], axis=1).astype(BF16)
    qk = QK_NOPE + QK_ROPE
    w_uq_p = jnp.pad(w_uq[l].reshape(Q_LORA, MLA_HEADS, qk),
                     ((0, 0), (0, 0), (0, LANES - qk))).reshape(Q_LORA, MLA_HEADS * LANES)
    kv = w_ukv[l].reshape(KV_LORA, MLA_HEADS, QK_NOPE + V_HEAD)
    wk, wv = kv[..., :QK_NOPE], kv[..., QK_NOPE:]
    w_uk_p = jnp.pad(wk, ((0, 0), (0, 0), (0, LANES - QK_NOPE))).reshape(KV_LORA, MLA_HEADS * LANES)
    wkt = wk.transpose(1, 2, 0)
    scale = qk ** -0.5
    vec = lambda v: v.reshape(1, -1)
    return {
        "ln1": vec(ln1[l]), "w_in": w_in_p, "g_qlat": vec(g_qlat[l]),
        "w_uq": w_uq_p.astype(BF16),
        "g_q": vec(jnp.concatenate([g_qn[l], g_qr[l], jnp.zeros(LANES - qk, F32)]) * scale),
        "g_kvlat": vec(g_kvlat[l]),
        "g_kr": vec(jnp.concatenate([jnp.zeros(ROPE_LO, F32), g_kr[l],
                                     jnp.zeros(LANES - ROPE_LO - QK_ROPE, F32)])),
        "g_k": vec(jnp.concatenate([g_kn[l], jnp.zeros(LANES - QK_NOPE, F32)])),
        "w_uk": w_uk_p.astype(BF16),
        "w_uvt": wv.transpose(1, 2, 0).reshape(MLA_HEADS * V_HEAD, KV_LORA).astype(BF16),
        "w_ukt": jnp.pad(wkt, ((0, 0), (0, LANES - QK_NOPE), (0, 0))).reshape(
            MLA_HEADS * LANES, KV_LORA).astype(BF16),
        "w_ukt_all": wkt.reshape(MLA_HEADS * QK_NOPE, KV_LORA).astype(BF16),
        "w_uv": wv.reshape(KV_LORA, MLA_HEADS * V_HEAD).astype(BF16),
        "gmat": jnp.asarray(_group_mean_matrix(), BF16),
        "g_ret": vec(g_ret[l]), "w_o": w_o[l].astype(BF16), "ln2": vec(ln2[l]),
        "w_up": w_up[l].astype(BF16), "w_down": w_down[l].astype(BF16),
    }


def _rope_tables(pos):
    def cos_sin(dim):
        inv = ROPE_THETA ** (-jnp.arange(0, dim, 2, dtype=F32) / dim)
        ang = pos.astype(F32)[:, None] * inv[None, :]
        return jnp.cos(ang), jnp.sin(ang)
    n = pos.shape[0]
    c_r, s_r = cos_sin(RET_QK)
    c_m, s_m = cos_sin(QK_ROPE)
    tail = jnp.zeros((n, LANES - ROPE_LO - QK_ROPE), F32)
    return {
        "cr": jnp.concatenate([c_r, c_r], axis=1),
        "sr": jnp.concatenate([-s_r, s_r], axis=1),
        "cm": jnp.concatenate([jnp.ones((n, ROPE_LO), F32), c_m, c_m, tail], axis=1),
        "sm": jnp.concatenate([jnp.zeros((n, ROPE_LO), F32), -s_m, s_m, tail], axis=1),
    }


def kernel(x_prompt, x_sample, cache_ckv, cache_krope, state_ret, page_table, ln1, w_in, g_qlat,
           w_uq, g_qn, g_qr, g_kvlat, g_kr, w_ukv, g_kn, g_ret, w_o, ln2, w_up, w_down):
    batch, seq, _ = x_prompt.shape
    dec_batch, dec_seq, _ = x_sample.shape
    depth = w_in.shape[0]
    n_pages = page_table.shape[1]
    past = n_pages * cache_ckv.shape[2]

    tm_p = min(256, seq)
    tm_s = min(256, dec_batch * dec_seq)
    tq = min(512, seq)
    ret_chunk = min(128, seq)
    chunk_pages = min(16, n_pages)
    bb = min(4, dec_batch)

    tabs_p = _rope_tables(jnp.arange(seq))
    tabs_s = {k: jnp.tile(v, (tm_s // dec_seq, 1))
              for k, v in _rope_tables(past + jnp.arange(dec_seq)).items()}
    lgtab = jnp.broadcast_to(
        jnp.log1p(-jnp.exp2(-5.0 - jnp.arange(RET_HEADS, dtype=F32)))[:, None, None],
        (RET_HEADS, 8, LANES))

    xp = x_prompt.reshape(batch * seq, D_MODEL)
    xs = x_sample.reshape(dec_batch * dec_seq, D_MODEL)
    outs = [[] for _ in range(6)]
    for l in range(depth):
        w = _prep_layer(l, ln1, w_in, g_qlat, w_uq, g_qn, g_qr, g_kvlat, g_kr, w_ukv, g_kn,
                        g_ret, w_o, ln2, w_up, w_down)

        q, ckv, kr, rq, rk, rv, rg, ga, gb, k, vt = _in_proj(
            "prompt", xp, w, tabs_p, seq // tm_p, tm_p, batch, seq)
        o_mla = _mla_prompt(q, k, vt, batch, seq, tq, tq)
        y, st = _ret_prompt(lgtab, rq, rk, rv, batch, seq, ret_chunk)
        xp = _out_proj(xp, o_mla, y, rg, ga, gb, w, tm_p)
        outs[0].append(ckv.reshape(batch, seq, KV_LORA))
        outs[1].append(kr.reshape(batch, seq, QK_ROPE))
        outs[2].append(st)

        q, ckv, kr, rq, rk, rv, rg, ga, gb, qabs = _in_proj(
            "sample", xs, w, tabs_s, 1, tm_s, dec_batch, dec_seq)
        o_mla = _mla_sample(l, page_table, q, qabs, ckv, kr, w["w_ukt_all"], w["w_uv"],
                            cache_ckv, cache_krope, chunk_pages)
        y, st = _ret_sample(l, lgtab, rq, rk, rv, state_ret, bb)
        xs = _out_proj(xs, o_mla, y, rg, ga, gb, w, tm_s)
        outs[3].append(ckv.reshape(dec_batch, dec_seq, KV_LORA))
        outs[4].append(kr.reshape(dec_batch, dec_seq, QK_ROPE))
        outs[5].append(st)

    return (xp.reshape(batch, seq, D_MODEL), xs.reshape(dec_batch, dec_seq, D_MODEL),
            jnp.stack(outs[0]), jnp.stack(outs[1]), jnp.stack(outs[2]),
            jnp.stack(outs[3]), jnp.stack(outs[4]), jnp.stack(outs[5]))
```

```python
import functools

import numpy as np
import jax
import jax.numpy as jnp
from jax import lax
from jax.experimental import pallas as pl
from jax.experimental.pallas import tpu as pltpu

F32 = jnp.float32
BF16 = jnp.bfloat16

D_MODEL = 1024
MLA_HEADS = 8
V_HEAD = 128
Q_LORA = 384
KV_LORA = 256
QK_NOPE = 64
QK_ROPE = 32
RET_HEADS = 4
RET_V = 256
RET_QK = 128
D_FF = 4096
ROPE_THETA = 10000.0
EPS = 1e-6

LANES = 128
VMEM_LIMIT = 52 << 20

_SEG = {}
_off = 0
for _name, _w in (("qlat", Q_LORA), ("kvlat", KV_LORA), ("rq", RET_HEADS * RET_QK),
                  ("rk", RET_HEADS * RET_QK), ("rv", RET_HEADS * RET_V), ("rg", D_MODEL),
                  ("ga", D_MODEL), ("gb", D_MODEL), ("kr", LANES)):
    _SEG[_name] = (_off, _off + _w)
    _off += _w
D_IN_P = _off
ROPE_LO = QK_NOPE
ROPE_HALF = QK_ROPE // 2


def _group_mean_matrix():
    gid = np.array([0] * QK_NOPE + [1] * QK_ROPE + [2] * (LANES - QK_NOPE - QK_ROPE))
    size = np.array([QK_NOPE, QK_ROPE, LANES - QK_NOPE - QK_ROPE], np.float32)
    g = (gid[:, None] == gid[None, :]).astype(np.float32) / size[gid][None, :]
    out = np.zeros((2 * LANES, 2 * LANES), np.float32)
    out[:LANES, :LANES] = g
    out[LANES:, LANES:] = g
    return out


def _const_spec(shape):
    nd = len(shape)
    return pl.BlockSpec(shape, lambda *_: (0,) * nd, pipeline_mode=pl.Buffered(1))


def _rmsnorm(x, g):
    return x * lax.rsqrt(jnp.mean(x * x, axis=-1, keepdims=True) + EPS) * g


def _sigmoid(x):
    return 1.0 / (1.0 + jnp.exp(-x))


def _dot(a, b):
    return jnp.dot(a, b, preferred_element_type=F32)


def _dot_nt(a, b):
    return lax.dot_general(a, b, (((1,), (1,)), ((), ())), preferred_element_type=F32)


def _head_norm(v, g_ref, gain2):
    heads = []
    for p in range(MLA_HEADS // 2):
        vp = v[:, p * 2 * LANES:(p + 1) * 2 * LANES]
        ms = _dot((vp * vp).astype(BF16), g_ref[...])
        n = vp * lax.rsqrt(ms + EPS) * gain2
        heads.append(n[:, :LANES])
        heads.append(n[:, LANES:])
    return heads


def _rope32(n, cm, sm):
    lane = lax.broadcasted_iota(jnp.int32, n.shape, 1)
    rot = jnp.where(lane < ROPE_LO + ROPE_HALF,
                    pltpu.roll(n, LANES - ROPE_HALF, axis=1),
                    pltpu.roll(n, ROPE_HALF, axis=1))
    return n * cm + rot * sm


def _in_kernel(mode, x_ref, ln1_ref, win_ref, gql_ref, wuq_ref, gq_ref, gkvl_ref, gkr_ref,
               g_ref, cr_ref, sr_ref, cm_ref, sm_ref, wk_ref, gk_ref, *rest):
    if mode == "prompt":
        wuvt_ref = rest[0]
        q_o, ckv_o, kr_o, rq_o, rk_o, rv_o, rg_o, ga_o, gb_o, k_o, vt_o = rest[1:]
    else:
        q_o, ckv_o, kr_o, rq_o, rk_o, rv_o, rg_o, ga_o, gb_o, qabs_o = rest

    h = _rmsnorm(x_ref[...], ln1_ref[...]).astype(BF16)

    def seg(name):
        a, b = _SEG[name]
        return _dot(h, win_ref[:, a:b])

    cm = cm_ref[...]
    sm = sm_ref[...]

    qn = _rmsnorm(seg("qlat"), gql_ref[...]).astype(BF16)
    q = _dot(qn, wuq_ref[...])
    gq2 = jnp.concatenate([gq_ref[...], gq_ref[...]], axis=1)
    q_heads = [_rope32(n, cm, sm) for n in _head_norm(q, g_ref, gq2)]
    for hh in range(MLA_HEADS):
        q_o[:, hh * LANES:(hh + 1) * LANES] = q_heads[hh].astype(q_o.dtype)

    c = _rmsnorm(seg("kvlat"), gkvl_ref[...])
    ckv_o[...] = c
    cb = c.astype(BF16)
    kr = seg("kr")
    ms = _dot((kr * kr).astype(BF16), g_ref[:LANES, :LANES])
    krr = _rope32(kr * lax.rsqrt(ms + EPS) * gkr_ref[...], cm, sm)
    kr_o[...] = krr[:, ROPE_LO:ROPE_LO + QK_ROPE]

    if mode == "prompt":
        gk2 = jnp.concatenate([gk_ref[...], gk_ref[...]], axis=1)
        k_heads = _head_norm(_dot(cb, wk_ref[...]), g_ref, gk2)
        for hh in range(MLA_HEADS):
            k_o[:, hh * LANES:(hh + 1) * LANES] = (k_heads[hh] + krr).astype(k_o.dtype)
        vt_o[...] = _dot_nt(wuvt_ref[...], cb).astype(vt_o.dtype)
    else:
        gk = gk_ref[...]
        for hh in range(MLA_HEADS):
            qg = (q_heads[hh] * gk).astype(BF16)
            qabs_o[:, hh * KV_LORA:(hh + 1) * KV_LORA] = _dot(
                qg, wk_ref[hh * LANES:(hh + 1) * LANES, :]).astype(qabs_o.dtype)

    cr = cr_ref[...]
    sr = sr_ref[...]
    rq = seg("rq")
    rk = seg("rk")
    for hh in range(RET_HEADS):
        sl = slice(hh * RET_QK, (hh + 1) * RET_QK)
        a = rq[:, sl]
        rq_o[:, sl] = (a * cr + pltpu.roll(a, RET_QK // 2, axis=1) * sr).astype(rq_o.dtype)
        b = rk[:, sl]
        rk_o[:, sl] = ((b * cr + pltpu.roll(b, RET_QK // 2, axis=1) * sr)
                       * (RET_QK ** -0.5)).astype(rk_o.dtype)
    rv_o[...] = seg("rv").astype(rv_o.dtype)
    rg_o[...] = seg("rg").astype(rg_o.dtype)
    ga_o[...] = seg("ga").astype(ga_o.dtype)
    gb_o[...] = seg("gb").astype(gb_o.dtype)


def _in_proj(mode, x, w, tabs, n_seq_blocks, tm, batch, seq):
    T = x.shape[0]
    nt = T // tm
    act = BF16 if mode == "prompt" else F32
    row = lambda width: pl.BlockSpec((tm, width), lambda i: (i, 0))
    tab = pl.BlockSpec((tm, LANES), lambda i: (i % n_seq_blocks, 0))
    in_specs = [row(D_MODEL), _const_spec((1, D_MODEL)), _const_spec((D_MODEL, D_IN_P)),
                _const_spec((1, Q_LORA)), _const_spec((Q_LORA, MLA_HEADS * LANES)),
                _const_spec((1, LANES)), _const_spec((1, KV_LORA)), _const_spec((1, LANES)),
                _const_spec((2 * LANES, 2 * LANES)), tab, tab, tab, tab]
    args = [x, w["ln1"], w["w_in"], w["g_qlat"], w["w_uq"], w["g_q"], w["g_kvlat"], w["g_kr"],
            w["gmat"], tabs["cr"], tabs["sr"], tabs["cm"], tabs["sm"]]
    out_shape = [jax.ShapeDtypeStruct((T, MLA_HEADS * LANES), act),
                 jax.ShapeDtypeStruct((T, KV_LORA), F32),
                 jax.ShapeDtypeStruct((T, QK_ROPE), F32),
                 jax.ShapeDtypeStruct((T, RET_HEADS * RET_QK), act),
                 jax.ShapeDtypeStruct((T, RET_HEADS * RET_QK), act),
                 jax.ShapeDtypeStruct((T, RET_HEADS * RET_V), act),
                 jax.ShapeDtypeStruct((T, D_MODEL), act),
                 jax.ShapeDtypeStruct((T, D_MODEL), act),
                 jax.ShapeDtypeStruct((T, D_MODEL), act)]
    out_specs = [row(MLA_HEADS * LANES), row(KV_LORA), row(QK_ROPE), row(RET_HEADS * RET_QK),
                 row(RET_HEADS * RET_QK), row(RET_HEADS * RET_V), row(D_MODEL), row(D_MODEL),
                 row(D_MODEL)]
    if mode == "prompt":
        in_specs += [_const_spec((KV_LORA, MLA_HEADS * LANES)), _const_spec((1, LANES)),
                     _const_spec((MLA_HEADS * V_HEAD, KV_LORA))]
        args += [w["w_uk"], w["g_k"], w["w_uvt"]]
        out_shape += [jax.ShapeDtypeStruct((T, MLA_HEADS * LANES), BF16),
                      jax.ShapeDtypeStruct((batch, MLA_HEADS * V_HEAD, seq), BF16)]
        out_specs += [row(MLA_HEADS * LANES),
                      pl.BlockSpec((None, MLA_HEADS * V_HEAD, tm),
                                   lambda i: (i // n_seq_blocks, 0, i % n_seq_blocks))]
    else:
        in_specs += [_const_spec((MLA_HEADS * LANES, KV_LORA)), _const_spec((1, LANES))]
        args += [w["w_ukt"], w["g_k"]]
        out_shape += [jax.ShapeDtypeStruct((T, MLA_HEADS * KV_LORA), F32)]
        out_specs += [row(MLA_HEADS * KV_LORA)]
    return pl.pallas_call(
        functools.partial(_in_kernel, mode),
        grid=(nt,), in_specs=in_specs, out_specs=out_specs, out_shape=out_shape,
        compiler_params=pltpu.CompilerParams(dimension_semantics=("parallel",),
                                             vmem_limit_bytes=VMEM_LIMIT),
        name="in_proj_" + mode,
    )(*args)


def _flash_kernel(tq, tk, nh, q_ref, k_ref, vt_ref, o_ref, m_ref, l_ref, acc_ref):
    qi = pl.program_id(2)
    m_ref[...] = jnp.full(m_ref.shape, -jnp.inf, F32)
    l_ref[...] = jnp.zeros(l_ref.shape, F32)
    acc_ref[...] = jnp.zeros(acc_ref.shape, F32)

    def block(ki, masked):
        k0 = pl.multiple_of(ki * tk, tk)
        if masked:
            kpos = k0 + lax.broadcasted_iota(jnp.int32, (tk, tq), 0)
            qpos = qi * tq + lax.broadcasted_iota(jnp.int32, (tk, tq), 1)
            keep = kpos <= qpos
        scores = []
        for hh in range(nh):
            hs = slice(hh * LANES, (hh + 1) * LANES)
            scores.append(_dot_nt(k_ref[pl.ds(k0, tk), hs], q_ref[:, hs]))
        for hh in range(nh):
            hs = slice(hh * LANES, (hh + 1) * LANES)
            s = scores[hh]
            if masked:
                s = jnp.where(keep, s, -jnp.inf)
            m_prev = m_ref[hh]
            m_new = jnp.maximum(m_prev, jnp.max(s, axis=0, keepdims=True))
            alpha = jnp.exp2(m_prev - m_new)
            p = jnp.exp2(s - m_new)
            l_ref[hh] = alpha * l_ref[hh] + jnp.sum(p, axis=0, keepdims=True)
            acc_ref[hh] = alpha * acc_ref[hh] + _dot(vt_ref[hs, pl.ds(k0, tk)], p.astype(BF16))
            m_ref[hh] = m_new

    n_full = (qi * tq) // tk

    def body(ki, carry):
        block(ki, False)
        return carry

    lax.fori_loop(0, n_full, body, 0)
    for d in range(tq // tk):
        block(n_full + d, True)
    for hh in range(nh):
        o_ref[:, hh * V_HEAD:(hh + 1) * V_HEAD] = (acc_ref[hh] / l_ref[hh]).T.astype(o_ref.dtype)


def _mla_prompt(q, k, vt, batch, seq, tq, tk, nh):
    nq = seq // tq
    return pl.pallas_call(
        functools.partial(_flash_kernel, tq, tk, nh),
        grid=(batch, MLA_HEADS // nh, nq),
        in_specs=[pl.BlockSpec((tq, nh * LANES), lambda b, h, i: (b * nq + i, h)),
                  pl.BlockSpec((seq, nh * LANES), lambda b, h, i: (b, h)),
                  pl.BlockSpec((None, nh * V_HEAD, seq), lambda b, h, i: (b, h, 0))],
        out_specs=pl.BlockSpec((tq, nh * V_HEAD), lambda b, h, i: (b * nq + i, h)),
        out_shape=jax.ShapeDtypeStruct((batch * seq, MLA_HEADS * V_HEAD), BF16),
        scratch_shapes=[pltpu.VMEM((nh, 1, tq), F32), pltpu.VMEM((nh, 1, tq), F32),
                        pltpu.VMEM((nh, V_HEAD, tq), F32)],
        compiler_params=pltpu.CompilerParams(
            dimension_semantics=("parallel", "parallel", "arbitrary"),
            vmem_limit_bytes=VMEM_LIMIT),
        name="mla_prompt",
    )(q, k, vt)


def _group_norm(o):
    mu = jnp.mean(o, axis=-1, keepdims=True)
    d = o - mu
    return d * lax.rsqrt(jnp.mean(d * d, axis=-1, keepdims=True) + EPS)


def _decay_tables(lg, c):
    ti = lax.broadcasted_iota(jnp.int32, (c, c), 0)
    tj = lax.broadcasted_iota(jnp.int32, (c, c), 1)
    diff = (ti - tj).astype(F32)
    decay = jnp.where(diff >= 0, jnp.exp(lg * jnp.maximum(diff, 0.0)), 0.0)
    t = lax.broadcasted_iota(jnp.int32, (c, 1), 0).astype(F32)
    cross = jnp.exp(lg * (t + 1.0))
    end = jnp.exp(lg * (c - 1.0 - t))
    total = jnp.exp(lg * float(c))
    return decay, cross, end, total


def _ret_prompt_kernel(chunk, n_chunks, lg_ref, rq_ref, rk_ref, rv_ref, y_ref, st_ref, s_ref):
    si = pl.program_id(1)
    tables = [_decay_tables(lg_ref[h][0:1, 0:1], chunk) for h in range(RET_HEADS)]

    @pl.when(si == 0)
    def _():
        s_ref[...] = jnp.zeros(s_ref.shape, F32)

    def body(ci, carry):
        r0 = pl.multiple_of(ci * chunk, chunk)
        for h in range(RET_HEADS):
            decay, cross, end, total = tables[h]
            q = rq_ref[pl.ds(r0, chunk), h * RET_QK:(h + 1) * RET_QK]
            k = rk_ref[pl.ds(r0, chunk), h * RET_QK:(h + 1) * RET_QK]
            v = rv_ref[pl.ds(r0, chunk), h * RET_V:(h + 1) * RET_V]
            state = s_ref[h]
            a = _dot_nt(q, k) * decay
            o = _dot(a.astype(BF16), v) + _dot(q, state.astype(BF16)) * cross
            kd = (k.astype(F32) * end).astype(BF16)
            s_ref[h] = total * state + lax.dot_general(
                kd, v, (((0,), (0,)), ((), ())), preferred_element_type=F32)
            y_ref[pl.ds(r0, chunk), h * RET_V:(h + 1) * RET_V] = _group_norm(o)
        return carry

    lax.fori_loop(0, n_chunks, body, 0)

    @pl.when(si == pl.num_programs(1) - 1)
    def _():
        st_ref[...] = s_ref[...]


def _ret_prompt(lgtab, rq, rk, rv, batch, seq, chunk, ts):
    ns = seq // ts
    row = lambda width: pl.BlockSpec((ts, width), lambda b, s: (b * ns + s, 0))
    return pl.pallas_call(
        functools.partial(_ret_prompt_kernel, chunk, ts // chunk),
        grid=(batch, ns),
        in_specs=[_const_spec((RET_HEADS, 8, LANES)), row(RET_HEADS * RET_QK),
                  row(RET_HEADS * RET_QK), row(RET_HEADS * RET_V)],
        out_specs=[row(RET_HEADS * RET_V),
                   pl.BlockSpec((None, RET_HEADS, RET_QK, RET_V), lambda b, s: (b, 0, 0, 0))],
        out_shape=[jax.ShapeDtypeStruct((batch * seq, RET_HEADS * RET_V), F32),
                   jax.ShapeDtypeStruct((batch, RET_HEADS, RET_QK, RET_V), F32)],
        scratch_shapes=[pltpu.VMEM((RET_HEADS, RET_QK, RET_V), F32)],
        compiler_params=pltpu.CompilerParams(dimension_semantics=("parallel", "arbitrary"),
                                             vmem_limit_bytes=VMEM_LIMIT),
        name="ret_prompt",
    )(lgtab, rq, rk, rv)


def _ret_sample_kernel(bb, t, lg_ref, rq_ref, rk_ref, rv_ref, st_ref, stack_hbm, y_ref, stn_ref):
    del stack_hbm
    for h in range(RET_HEADS):
        lg = lg_ref[h][0:1, 0:1]
        decay, cross, end, total = _decay_tables(lg, t)
        for bi in range(bb):
            rows = slice(bi * t, (bi + 1) * t)
            q = rq_ref[rows, h * RET_QK:(h + 1) * RET_QK].astype(BF16)
            k = rk_ref[rows, h * RET_QK:(h + 1) * RET_QK]
            v = rv_ref[rows, h * RET_V:(h + 1) * RET_V].astype(BF16).astype(F32)
            state = st_ref[bi, h]
            a = (_dot_nt(q, k.astype(BF16)) * decay).astype(BF16).astype(F32)
            o = _dot(q, state.astype(BF16)) * cross
            for j in range(t):
                o = o + a[:, j:j + 1] * v[j:j + 1, :]
            y_ref[rows, h * RET_V:(h + 1) * RET_V] = _group_norm(o)
            kd = (k * end).astype(BF16).astype(F32)
            stn_ref[bi, h] = total * state + _dot(kd.T, v)


def _ret_sample(layer, lgtab, rq, rk, rv, state, stacked, bb):
    dec_batch = state.shape[1]
    t = rq.shape[0] // dec_batch
    row = lambda width: pl.BlockSpec((bb * t, width), lambda i: (i, 0))
    st_spec = pl.BlockSpec((None, bb, RET_HEADS, RET_QK, RET_V), lambda i: (layer, i, 0, 0, 0))
    return pl.pallas_call(
        functools.partial(_ret_sample_kernel, bb, t),
        grid=(dec_batch // bb,),
        in_specs=[_const_spec((RET_HEADS, 8, LANES)), row(RET_HEADS * RET_QK),
                  row(RET_HEADS * RET_QK), row(RET_HEADS * RET_V), st_spec,
                  pl.BlockSpec(memory_space=pl.ANY)],
        out_specs=[row(RET_HEADS * RET_V), st_spec],
        out_shape=[jax.ShapeDtypeStruct((dec_batch * t, RET_HEADS * RET_V), F32),
                   jax.ShapeDtypeStruct(state.shape, F32)],
        input_output_aliases={5: 1},
        compiler_params=pltpu.CompilerParams(dimension_semantics=("parallel",),
                                             vmem_limit_bytes=VMEM_LIMIT),
        name="ret_sample",
    )(lgtab, rq, rk, rv, state, stacked)


def _sattn_kernel(layer, n_pages, chunk_pages, t, pt_ref, q_ref, qabs_ref, cnew_ref, krnew_ref,
                  wkt_ref, wv_ref, cc_hbm, ckrt_hbm, o_ref, cbuf, krbuf, sem):
    b = pl.program_id(0)
    nb = pl.num_programs(0)
    slot = lax.rem(b, 2)
    page = cbuf.shape[2]
    hq = MLA_HEADS * t

    def start_fetch(bb, sl, lo, hi):
        for p in range(lo, hi):
            pg = pt_ref[bb, p]
            pltpu.make_async_copy(cc_hbm.at[layer, pg], cbuf.at[sl, p], sem.at[0, sl]).start()
            pltpu.make_async_copy(ckrt_hbm.at[layer, pg], krbuf.at[sl, p], sem.at[1, sl]).start()

    def wait_fetch(sl):
        pages = pl.ds(0, n_pages)
        pltpu.make_async_copy(cc_hbm.at[layer, pages], cbuf.at[sl], sem.at[0, sl]).wait()
        pltpu.make_async_copy(ckrt_hbm.at[layer, pages], krbuf.at[sl], sem.at[1, sl]).wait()

    @pl.when(b == 0)
    def _():
        start_fetch(0, 0, 0, n_pages)

    wait_fetch(slot)
    b_next = lax.rem(b + 1, nb)

    qh = q_ref[...]
    qr = jnp.concatenate(
        [qh[:, h * LANES + ROPE_LO:h * LANES + ROPE_LO + QK_ROPE] for h in range(MLA_HEADS)],
        axis=0).astype(BF16)
    qa = jnp.concatenate(
        [qabs_ref[:, h * KV_LORA:(h + 1) * KV_LORA] for h in range(MLA_HEADS)],
        axis=0).astype(BF16)
    lhs = jnp.concatenate([wkt_ref[...], qa], axis=0)
    n_exp = MLA_HEADS * QK_NOPE

    def scores(c, s_rope, mask):
        n = c.shape[0]
        cb = c.astype(BF16)
        big = _dot_nt(lhs, cb)
        kt = big[:n_exp].reshape(MLA_HEADS, QK_NOPE, n)
        inv = lax.rsqrt(jnp.mean(kt * kt, axis=1, keepdims=True) + EPS)
        s = big[n_exp:].reshape(MLA_HEADS, t, n) * inv + s_rope.reshape(MLA_HEADS, t, n)
        if mask is not None:
            s = jnp.where(mask, s, -jnp.inf)
        return s, cb

    def update(carry, s, cb):
        m, l, acc = carry
        n = s.shape[-1]
        m_new = jnp.maximum(m, jnp.max(s, axis=-1, keepdims=True))
        alpha = jnp.exp2(m - m_new)
        p = jnp.exp2(s - m_new)
        l = alpha * l + jnp.sum(p, axis=-1, keepdims=True)
        acc = alpha.reshape(hq, 1) * acc + _dot(p.reshape(hq, n).astype(BF16), cb)
        return m_new, l, acc

    carry = (jnp.full((MLA_HEADS, t, 1), -jnp.inf, F32), jnp.zeros((MLA_HEADS, t, 1), F32),
             jnp.zeros((hq, KV_LORA), F32))
    n_keys = chunk_pages * page
    pending = None
    for j in range(n_pages // chunk_pages):
        c = cbuf[slot, pl.ds(j * chunk_pages, chunk_pages)].reshape(n_keys, KV_LORA)
        krt = jnp.concatenate(
            [krbuf[slot, j * chunk_pages + i] for i in range(chunk_pages)], axis=1)
        nxt = scores(c, _dot(qr, krt.astype(BF16)), None)
        start_fetch(b_next, 1 - slot, j * chunk_pages, (j + 1) * chunk_pages)
        if pending is not None:
            carry = update(carry, *pending)
        pending = nxt

    pad = LANES - t
    c = jnp.concatenate([cnew_ref[...], jnp.zeros((pad, KV_LORA), F32)], axis=0)
    kr = jnp.concatenate([krnew_ref[...], jnp.zeros((pad, QK_ROPE), F32)], axis=0)
    shape = (MLA_HEADS, t, LANES)
    mask = lax.broadcasted_iota(jnp.int32, shape, 2) <= lax.broadcasted_iota(jnp.int32, shape, 1)
    last = scores(c, _dot_nt(qr, kr.astype(BF16)), mask)
    carry = update(carry, *pending)
    m, l, acc = update(carry, *last)

    @pl.when(b == nb - 1)
    def _():
        wait_fetch(1 - slot)

    oc = (acc / l.reshape(hq, 1)).astype(BF16)
    full = _dot(oc, wv_ref[...])
    for h in range(MLA_HEADS):
        o_ref[:, h * V_HEAD:(h + 1) * V_HEAD] = full[h * t:(h + 1) * t, h * V_HEAD:(h + 1) * V_HEAD]


def _mla_sample(layer, page_table, q, qabs, cnew, krnew, wkt, wv, cache_ckv, cache_krope_t,
                chunk_pages):
    dec_batch, n_pages = page_table.shape
    t = q.shape[0] // dec_batch
    page = cache_ckv.shape[2]
    row = lambda width: pl.BlockSpec((t, width), lambda b, pt: (b, 0))
    const = lambda shape: pl.BlockSpec(shape, lambda b, pt: (0,) * len(shape),
                                       pipeline_mode=pl.Buffered(1))
    grid_spec = pltpu.PrefetchScalarGridSpec(
        num_scalar_prefetch=1, grid=(dec_batch,),
        in_specs=[row(MLA_HEADS * LANES), row(MLA_HEADS * KV_LORA), row(KV_LORA), row(QK_ROPE),
                  const((MLA_HEADS * QK_NOPE, KV_LORA)), const((KV_LORA, MLA_HEADS * V_HEAD)),
                  pl.BlockSpec(memory_space=pl.ANY), pl.BlockSpec(memory_space=pl.ANY)],
        out_specs=row(MLA_HEADS * V_HEAD),
        scratch_shapes=[pltpu.VMEM((2, n_pages, page, KV_LORA), F32),
                        pltpu.VMEM((2, n_pages, QK_ROPE, page), F32),
                        pltpu.SemaphoreType.DMA((2, 2))])
    return pl.pallas_call(
        functools.partial(_sattn_kernel, layer, n_pages, chunk_pages, t),
        grid_spec=grid_spec,
        out_shape=jax.ShapeDtypeStruct((dec_batch * t, MLA_HEADS * V_HEAD), F32),
        compiler_params=pltpu.CompilerParams(dimension_semantics=("arbitrary",),
                                             vmem_limit_bytes=VMEM_LIMIT),
        name="mla_sample",
    )(page_table, q, qabs, cnew, krnew, wkt, wv, cache_ckv, cache_krope_t)


def _out_kernel(x_ref, om_ref, y_ref, rg_ref, ga_ref, gb_ref, gret_ref, wo_ref, ln2_ref,
                wup_ref, wdn_ref, o_ref):
    rg = rg_ref[...].astype(F32)
    o_ret = y_ref[...] * gret_ref[...] * (rg * _sigmoid(rg))
    mixed = (_sigmoid(ga_ref[...].astype(F32)) * om_ref[...].astype(F32)
             + _sigmoid(gb_ref[...].astype(F32)) * o_ret)
    x1 = x_ref[...] + _dot(mixed.astype(BF16), wo_ref[...])
    h2 = _rmsnorm(x1, ln2_ref[...]).astype(BF16)
    u = jnp.maximum(_dot(h2, wup_ref[...]), 0.0)
    o_ref[...] = x1 + _dot((u * u).astype(BF16), wdn_ref[...])


def _out_proj(x, om, y, rg, ga, gb, w, tm):
    T = x.shape[0]
    row = pl.BlockSpec((tm, D_MODEL), lambda i: (i, 0))
    return pl.pallas_call(
        _out_kernel,
        grid=(T // tm,),
        in_specs=[row, row, row, row, row, row, _const_spec((1, D_MODEL)),
                  _const_spec((D_MODEL, D_MODEL)), _const_spec((1, D_MODEL)),
                  _const_spec((D_MODEL, D_FF)), _const_spec((D_FF, D_MODEL))],
        out_specs=row,
        out_shape=jax.ShapeDtypeStruct((T, D_MODEL), F32),
        compiler_params=pltpu.CompilerParams(dimension_semantics=("parallel",),
                                             vmem_limit_bytes=VMEM_LIMIT),
        name="out_proj",
    )(x, om, y, rg, ga, gb, w["g_ret"], w["w_o"], w["ln2"], w["w_up"], w["w_down"])


def _prep_layer(l, ln1, w_in, g_qlat, w_uq, g_qn, g_qr, g_kvlat, g_kr, w_ukv, g_kn, g_ret,
                w_o, ln2, w_up, w_down):
    wi = w_in[l]
    lat = Q_LORA + KV_LORA
    zeros = lambda n: jnp.zeros((D_MODEL, n), F32)
    w_in_p = jnp.concatenate(
        [wi[:, :lat], wi[:, lat + QK_ROPE:], zeros(ROPE_LO), wi[:, lat:lat + QK_ROPE],
         zeros(LANES - ROPE_LO - QK_ROPE)], axis=1).astype(BF16)
    qk = QK_NOPE + QK_ROPE
    w_uq_p = jnp.pad(w_uq[l].reshape(Q_LORA, MLA_HEADS, qk),
                     ((0, 0), (0, 0), (0, LANES - qk))).reshape(Q_LORA, MLA_HEADS * LANES)
    kv = w_ukv[l].reshape(KV_LORA, MLA_HEADS, QK_NOPE + V_HEAD)
    wk, wv = kv[..., :QK_NOPE], kv[..., QK_NOPE:]
    w_uk_p = jnp.pad(wk, ((0, 0), (0, 0), (0, LANES - QK_NOPE))).reshape(KV_LORA, MLA_HEADS * LANES)
    wkt = wk.transpose(1, 2, 0)
    scale = float(qk ** -0.5 * np.log2(np.e))
    vec = lambda v: v.reshape(1, -1)
    return {
        "ln1": vec(ln1[l]), "w_in": w_in_p, "g_qlat": vec(g_qlat[l]),
        "w_uq": w_uq_p.astype(BF16),
        "g_q": vec(jnp.concatenate([g_qn[l], g_qr[l], jnp.zeros(LANES - qk, F32)]) * scale),
        "g_kvlat": vec(g_kvlat[l]),
        "g_kr": vec(jnp.concatenate([jnp.zeros(ROPE_LO, F32), g_kr[l],
                                     jnp.zeros(LANES - ROPE_LO - QK_ROPE, F32)])),
        "g_k": vec(jnp.concatenate([g_kn[l], jnp.zeros(LANES - QK_NOPE, F32)])),
        "w_uk": w_uk_p.astype(BF16),
        "w_uvt": wv.transpose(1, 2, 0).reshape(MLA_HEADS * V_HEAD, KV_LORA).astype(BF16),
        "w_ukt": jnp.pad(wkt, ((0, 0), (0, LANES - QK_NOPE), (0, 0))).reshape(
            MLA_HEADS * LANES, KV_LORA).astype(BF16),
        "w_ukt_all": wkt.reshape(MLA_HEADS * QK_NOPE, KV_LORA).astype(BF16),
        "w_uv": wv.reshape(KV_LORA, MLA_HEADS * V_HEAD).astype(BF16),
        "gmat": jnp.asarray(_group_mean_matrix(), BF16),
        "g_ret": vec(g_ret[l]), "w_o": w_o[l].astype(BF16), "ln2": vec(ln2[l]),
        "w_up": w_up[l].astype(BF16), "w_down": w_down[l].astype(BF16),
    }


def _rope_tables(pos):
    def cos_sin(dim):
        inv = ROPE_THETA ** (-jnp.arange(0, dim, 2, dtype=F32) / dim)
        ang = pos.astype(F32)[:, None] * inv[None, :]
        return jnp.cos(ang), jnp.sin(ang)
    n = pos.shape[0]
    c_r, s_r = cos_sin(RET_QK)
    c_m, s_m = cos_sin(QK_ROPE)
    tail = jnp.zeros((n, LANES - ROPE_LO - QK_ROPE), F32)
    return {
        "cr": jnp.concatenate([c_r, c_r], axis=1),
        "sr": jnp.concatenate([-s_r, s_r], axis=1),
        "cm": jnp.concatenate([jnp.ones((n, ROPE_LO), F32), c_m, c_m, tail], axis=1),
        "sm": jnp.concatenate([jnp.zeros((n, ROPE_LO), F32), -s_m, s_m, tail], axis=1),
    }


def kernel(x_prompt, x_sample, cache_ckv, cache_krope, state_ret, page_table, ln1, w_in, g_qlat,
           w_uq, g_qn, g_qr, g_kvlat, g_kr, w_ukv, g_kn, g_ret, w_o, ln2, w_up, w_down):
    batch, seq, _ = x_prompt.shape
    dec_batch, dec_seq, _ = x_sample.shape
    depth = w_in.shape[0]
    n_pages = page_table.shape[1]
    past = n_pages * cache_ckv.shape[2]

    tm_p = min(256, seq)
    tm_s = min(256, dec_batch * dec_seq)
    tq = min(512, seq)
    ret_chunk = min(128, seq)
    ret_ts = min(1024, seq)
    chunk_pages = min(16, n_pages)
    bb = min(4, dec_batch)
    cache_krope_t = jnp.swapaxes(cache_krope, 2, 3)
    st_sample = jnp.zeros(state_ret.shape, F32)

    tabs_p = _rope_tables(jnp.arange(seq))
    tabs_s = {k: jnp.tile(v, (tm_s // dec_seq, 1))
              for k, v in _rope_tables(past + jnp.arange(dec_seq)).items()}
    lgtab = jnp.broadcast_to(
        jnp.log1p(-jnp.exp2(-5.0 - jnp.arange(RET_HEADS, dtype=F32)))[:, None, None],
        (RET_HEADS, 8, LANES))

    xp = x_prompt.reshape(batch * seq, D_MODEL)
    xs = x_sample.reshape(dec_batch * dec_seq, D_MODEL)
    outs = [[] for _ in range(5)]
    for l in range(depth):
        w = _prep_layer(l, ln1, w_in, g_qlat, w_uq, g_qn, g_qr, g_kvlat, g_kr, w_ukv, g_kn,
                        g_ret, w_o, ln2, w_up, w_down)

        q, ckv, kr, rq, rk, rv, rg, ga, gb, k, vt = _in_proj(
            "prompt", xp, w, tabs_p, seq // tm_p, tm_p, batch, seq)
        o_mla = _mla_prompt(q, k, vt, batch, seq, tq, tq, 4)
        y, st = _ret_prompt(lgtab, rq, rk, rv, batch, seq, ret_chunk, ret_ts)
        xp = _out_proj(xp, o_mla, y, rg, ga, gb, w, tm_p)
        outs[0].append(ckv.reshape(batch, seq, KV_LORA))
        outs[1].append(kr.reshape(batch, seq, QK_ROPE))
        outs[2].append(st)

        q, ckv, kr, rq, rk, rv, rg, ga, gb, qabs = _in_proj(
            "sample", xs, w, tabs_s, 1, tm_s, dec_batch, dec_seq)
        o_mla = _mla_sample(l, page_table, q, qabs, ckv, kr, w["w_ukt_all"], w["w_uv"],
                            cache_ckv, cache_krope_t, chunk_pages)
        y, st_sample = _ret_sample(l, lgtab, rq, rk, rv, state_ret, st_sample, bb)
        xs = _out_proj(xs, o_mla, y, rg, ga, gb, w, tm_s)
        outs[3].append(ckv.reshape(dec_batch, dec_seq, KV_LORA))
        outs[4].append(kr.reshape(dec_batch, dec_seq, QK_ROPE))

    return (xp.reshape(batch, seq, D_MODEL), xs.reshape(dec_batch, dec_seq, D_MODEL),
            jnp.stack(outs[0]), jnp.stack(outs[1]), jnp.stack(outs[2]),
            jnp.stack(outs[3]), jnp.stack(outs[4]), st_sample)
```

```python
import functools

import numpy as np
import jax
import jax.numpy as jnp
from jax import lax
from jax.experimental import pallas as pl
from jax.experimental.pallas import tpu as pltpu

F32 = jnp.float32
BF16 = jnp.bfloat16

D_MODEL = 1024
MLA_HEADS = 8
V_HEAD = 128
Q_LORA = 384
KV_LORA = 256
QK_NOPE = 64
QK_ROPE = 32
RET_HEADS = 4
RET_V = 256
RET_QK = 128
D_FF = 4096
ROPE_THETA = 10000.0
EPS = 1e-6

LANES = 128
VMEM_LIMIT = 52 << 20
SAMPLE_SLOTS = 4

_SEG = {}
_off = 0
for _name, _w in (("qlat", Q_LORA), ("kvlat", KV_LORA), ("rq", RET_HEADS * RET_QK),
                  ("rk", RET_HEADS * RET_QK), ("rv", RET_HEADS * RET_V), ("rg", D_MODEL),
                  ("ga", D_MODEL), ("gb", D_MODEL), ("kr", LANES)):
    _SEG[_name] = (_off, _off + _w)
    _off += _w
D_IN_P = _off
ROPE_LO = QK_NOPE
ROPE_HALF = QK_ROPE // 2


def _group_mean_matrix():
    gid = np.array([0] * QK_NOPE + [1] * QK_ROPE + [2] * (LANES - QK_NOPE - QK_ROPE))
    size = np.array([QK_NOPE, QK_ROPE, LANES - QK_NOPE - QK_ROPE], np.float32)
    g = (gid[:, None] == gid[None, :]).astype(np.float32) / size[gid][None, :]
    out = np.zeros((2 * LANES, 2 * LANES), np.float32)
    out[:LANES, :LANES] = g
    out[LANES:, LANES:] = g
    return out


def _const_spec(shape):
    nd = len(shape)
    return pl.BlockSpec(shape, lambda *_: (0,) * nd, pipeline_mode=pl.Buffered(1))


def _layer_spec(layer, shape):
    nd = len(shape)
    return pl.BlockSpec((None,) + tuple(shape), lambda *_: (layer,) + (0,) * nd,
                        pipeline_mode=pl.Buffered(1))


def _rmsnorm(x, g):
    return x * lax.rsqrt(jnp.mean(x * x, axis=-1, keepdims=True) + EPS) * g


def _sigmoid(x):
    return 1.0 / (1.0 + jnp.exp(-x))


def _dot(a, b):
    return jnp.dot(a, b, preferred_element_type=F32)


def _dot_nt(a, b):
    return lax.dot_general(a, b, (((1,), (1,)), ((), ())), preferred_element_type=F32)


def _head_norm(v, g_ref, gain2):
    heads = []
    for p in range(MLA_HEADS // 2):
        vp = v[:, p * 2 * LANES:(p + 1) * 2 * LANES]
        ms = _dot((vp * vp).astype(BF16), g_ref[...])
        n = vp * lax.rsqrt(ms + EPS) * gain2
        heads.append(n[:, :LANES])
        heads.append(n[:, LANES:])
    return heads


def _rope32(n, cm, sm):
    lane = lax.broadcasted_iota(jnp.int32, n.shape, 1)
    rot = jnp.where(lane < ROPE_LO + ROPE_HALF,
                    pltpu.roll(n, LANES - ROPE_HALF, axis=1),
                    pltpu.roll(n, ROPE_HALF, axis=1))
    return n * cm + rot * sm


def _in_kernel(mode, x_ref, ln1_ref, win_ref, gql_ref, wuq_ref, gq_ref, gkvl_ref, gkr_ref,
               g_ref, cr_ref, sr_ref, cm_ref, sm_ref, wk_ref, gk_ref, *rest):
    if mode == "prompt":
        wuvt_ref = rest[0]
        q_o, ckv_o, kr_o, rq_o, rk_o, rv_o, rg_o, ga_o, gb_o, k_o, vt_o = rest[1:]
    else:
        q_o, ckv_o, kr_o, rq_o, rk_o, rv_o, rg_o, ga_o, gb_o, qabs_o = rest

    h = _rmsnorm(x_ref[...], ln1_ref[...]).astype(BF16)

    def seg(name):
        a, b = _SEG[name]
        return _dot(h, win_ref[:, a:b])

    cm = cm_ref[...]
    sm = sm_ref[...]

    qn = _rmsnorm(seg("qlat"), gql_ref[...]).astype(BF16)
    q = _dot(qn, wuq_ref[...])
    gq2 = jnp.concatenate([gq_ref[...], gq_ref[...]], axis=1)
    q_heads = [_rope32(n, cm, sm) for n in _head_norm(q, g_ref, gq2)]
    for hh in range(MLA_HEADS):
        q_o[:, hh * LANES:(hh + 1) * LANES] = q_heads[hh].astype(q_o.dtype)

    c = _rmsnorm(seg("kvlat"), gkvl_ref[...])
    ckv_o[...] = c
    cb = c.astype(BF16)
    kr = seg("kr")
    ms = _dot((kr * kr).astype(BF16), g_ref[:LANES, :LANES])
    krr = _rope32(kr * lax.rsqrt(ms + EPS) * gkr_ref[...], cm, sm)
    kr_o[...] = krr[:, ROPE_LO:ROPE_LO + QK_ROPE]

    if mode == "prompt":
        gk2 = jnp.concatenate([gk_ref[...], gk_ref[...]], axis=1)
        k_heads = _head_norm(_dot(cb, wk_ref[...]), g_ref, gk2)
        for hh in range(MLA_HEADS):
            k_o[:, hh * LANES:(hh + 1) * LANES] = (k_heads[hh] + krr).astype(k_o.dtype)
        vt_o[...] = _dot_nt(wuvt_ref[...], cb).astype(vt_o.dtype)
    else:
        gk = gk_ref[...]
        for hh in range(MLA_HEADS):
            qg = (q_heads[hh] * gk).astype(BF16)
            qabs_o[:, hh * KV_LORA:(hh + 1) * KV_LORA] = _dot(
                qg, wk_ref[hh * LANES:(hh + 1) * LANES, :]).astype(qabs_o.dtype)

    cr = cr_ref[...]
    sr = sr_ref[...]
    rq = seg("rq")
    rk = seg("rk")
    for hh in range(RET_HEADS):
        sl = slice(hh * RET_QK, (hh + 1) * RET_QK)
        a = rq[:, sl]
        rq_o[:, sl] = (a * cr + pltpu.roll(a, RET_QK // 2, axis=1) * sr).astype(rq_o.dtype)
        b = rk[:, sl]
        rk_o[:, sl] = ((b * cr + pltpu.roll(b, RET_QK // 2, axis=1) * sr)
                       * (RET_QK ** -0.5)).astype(rk_o.dtype)
    rv_o[...] = seg("rv").astype(rv_o.dtype)
    rg_o[...] = seg("rg").astype(rg_o.dtype)
    ga_o[...] = seg("ga").astype(ga_o.dtype)
    gb_o[...] = seg("gb").astype(gb_o.dtype)


def _in_proj(mode, layer, x, w, tabs, n_seq_blocks, tm, batch, seq):
    T = x.shape[0]
    lspec = functools.partial(_layer_spec, layer)
    nt = T // tm
    act = BF16 if mode == "prompt" else F32
    row = lambda width: pl.BlockSpec((tm, width), lambda i: (i, 0))
    tab = pl.BlockSpec((tm, LANES), lambda i: (i % n_seq_blocks, 0))
    in_specs = [row(D_MODEL), lspec((1, D_MODEL)), lspec((D_MODEL, D_IN_P)),
                lspec((1, Q_LORA)), lspec((Q_LORA, MLA_HEADS * LANES)),
                lspec((1, LANES)), lspec((1, KV_LORA)), lspec((1, LANES)),
                _const_spec((2 * LANES, 2 * LANES)), tab, tab, tab, tab]
    args = [x, w["ln1"], w["w_in"], w["g_qlat"], w["w_uq"], w["g_q"], w["g_kvlat"], w["g_kr"],
            w["gmat"], tabs["cr"], tabs["sr"], tabs["cm"], tabs["sm"]]
    out_shape = [jax.ShapeDtypeStruct((T, MLA_HEADS * LANES), act),
                 jax.ShapeDtypeStruct((T, KV_LORA), F32),
                 jax.ShapeDtypeStruct((T, QK_ROPE), F32),
                 jax.ShapeDtypeStruct((T, RET_HEADS * RET_QK), act),
                 jax.ShapeDtypeStruct((T, RET_HEADS * RET_QK), act),
                 jax.ShapeDtypeStruct((T, RET_HEADS * RET_V), act),
                 jax.ShapeDtypeStruct((T, D_MODEL), act),
                 jax.ShapeDtypeStruct((T, D_MODEL), act),
                 jax.ShapeDtypeStruct((T, D_MODEL), act)]
    out_specs = [row(MLA_HEADS * LANES), row(KV_LORA), row(QK_ROPE), row(RET_HEADS * RET_QK),
                 row(RET_HEADS * RET_QK), row(RET_HEADS * RET_V), row(D_MODEL), row(D_MODEL),
                 row(D_MODEL)]
    if mode == "prompt":
        in_specs += [lspec((KV_LORA, MLA_HEADS * LANES)), lspec((1, LANES)),
                     lspec((MLA_HEADS * V_HEAD, KV_LORA))]
        args += [w["w_uk"], w["g_k"], w["w_uvt"]]
        out_shape += [jax.ShapeDtypeStruct((T, MLA_HEADS * LANES), BF16),
                      jax.ShapeDtypeStruct((batch, MLA_HEADS * V_HEAD, seq), BF16)]
        out_specs += [row(MLA_HEADS * LANES),
                      pl.BlockSpec((None, MLA_HEADS * V_HEAD, tm),
                                   lambda i: (i // n_seq_blocks, 0, i % n_seq_blocks))]
    else:
        in_specs += [lspec((MLA_HEADS * LANES, KV_LORA)), lspec((1, LANES))]
        args += [w["w_ukt"], w["g_k"]]
        out_shape += [jax.ShapeDtypeStruct((T, MLA_HEADS * KV_LORA), F32)]
        out_specs += [row(MLA_HEADS * KV_LORA)]
    return pl.pallas_call(
        functools.partial(_in_kernel, mode),
        grid=(nt,), in_specs=in_specs, out_specs=out_specs, out_shape=out_shape,
        compiler_params=pltpu.CompilerParams(dimension_semantics=("parallel",),
                                             vmem_limit_bytes=VMEM_LIMIT),
        name="in_proj_" + mode,
    )(*args)


def _flash_kernel(t, nh, q_ref, k_ref, vt_ref, o_ref, s_ref, bm_ref, m_ref, l_ref, acc_ref):
    qi = pl.program_id(2)
    m_ref[...] = jnp.full(m_ref.shape, -jnp.inf, F32)
    l_ref[...] = jnp.zeros(l_ref.shape, F32)
    acc_ref[...] = jnp.zeros(acc_ref.shape, F32)

    def step(cur, nxt, masked):
        if masked:
            kpos = nxt * t + lax.broadcasted_iota(jnp.int32, (t, t), 0)
            qpos = qi * t + lax.broadcasted_iota(jnp.int32, (t, t), 1)
            keep = kpos <= qpos
        for hh in range(nh):
            hs = slice(hh * LANES, (hh + 1) * LANES)
            if cur is not None:
                s_cur = s_ref[hh]
                m_prev = m_ref[hh]
                m_new = jnp.maximum(m_prev, bm_ref[hh])
                alpha = jnp.exp2(m_prev - m_new)
            if nxt is not None:
                n0 = pl.multiple_of(nxt * t, t)
                s = _dot_nt(k_ref[pl.ds(n0, t), hs], q_ref[:, hs])
                if masked:
                    s = jnp.where(keep, s, -jnp.inf)
                s_ref[hh] = s
                bm_ref[hh] = jnp.max(s, axis=0, keepdims=True)
            if cur is not None:
                c0 = pl.multiple_of(cur * t, t)
                p = jnp.exp2(s_cur - m_new)
                l_ref[hh] = alpha * l_ref[hh] + jnp.sum(p, axis=0, keepdims=True)
                acc_ref[hh] = alpha * acc_ref[hh] + _dot(vt_ref[hs, pl.ds(c0, t)], p.astype(BF16))
                m_ref[hh] = m_new

    @pl.when(qi == 0)
    def _():
        step(None, 0, True)

    @pl.when(qi > 0)
    def _():
        step(None, 0, False)

        def body(i, carry):
            step(i, i + 1, False)
            return carry

        lax.fori_loop(0, qi - 1, body, 0)
        step(qi - 1, qi, True)

    step(qi, None, False)
    for hh in range(nh):
        o_ref[:, hh * V_HEAD:(hh + 1) * V_HEAD] = (acc_ref[hh] / l_ref[hh]).T.astype(o_ref.dtype)


def _mla_prompt(q, k, vt, batch, seq, t, nh):
    nq = seq // t
    stat = pltpu.VMEM((nh, 1, t), F32)
    return pl.pallas_call(
        functools.partial(_flash_kernel, t, nh),
        grid=(batch, MLA_HEADS // nh, nq),
        in_specs=[pl.BlockSpec((t, nh * LANES), lambda b, h, i: (b * nq + i, h)),
                  pl.BlockSpec((seq, nh * LANES), lambda b, h, i: (b, h)),
                  pl.BlockSpec((None, nh * V_HEAD, seq), lambda b, h, i: (b, h, 0))],
        out_specs=pl.BlockSpec((t, nh * V_HEAD), lambda b, h, i: (b * nq + i, h)),
        out_shape=jax.ShapeDtypeStruct((batch * seq, MLA_HEADS * V_HEAD), BF16),
        scratch_shapes=[pltpu.VMEM((nh, t, t), F32), stat, stat, stat,
                        pltpu.VMEM((nh, V_HEAD, t), F32)],
        compiler_params=pltpu.CompilerParams(
            dimension_semantics=("parallel", "parallel", "arbitrary"),
            vmem_limit_bytes=VMEM_LIMIT),
        name="mla_prompt",
    )(q, k, vt)


def _group_norm(o):
    mu = jnp.mean(o, axis=-1, keepdims=True)
    d = o - mu
    return d * lax.rsqrt(jnp.mean(d * d, axis=-1, keepdims=True) + EPS)


def _decay_tables(lg, c):
    ti = lax.broadcasted_iota(jnp.int32, (c, c), 0)
    tj = lax.broadcasted_iota(jnp.int32, (c, c), 1)
    diff = (ti - tj).astype(F32)
    decay = jnp.where(diff >= 0, jnp.exp(lg * jnp.maximum(diff, 0.0)), 0.0)
    t = lax.broadcasted_iota(jnp.int32, (c, 1), 0).astype(F32)
    cross = jnp.exp(lg * (t + 1.0))
    end = jnp.exp(lg * (c - 1.0 - t))
    total = jnp.exp(lg * float(c))
    return decay, cross, end, total


def _ret_prompt_kernel(chunk, n_chunks, lg_ref, rq_ref, rk_ref, rv_ref, y_ref, st_ref, s_ref):
    si = pl.program_id(1)
    tables = [_decay_tables(lg_ref[h][0:1, 0:1], chunk) for h in range(RET_HEADS)]

    @pl.when(si == 0)
    def _():
        s_ref[...] = jnp.zeros(s_ref.shape, F32)

    def body(ci, carry):
        r0 = pl.multiple_of(ci * chunk, chunk)
        for h in range(RET_HEADS):
            decay, cross, end, total = tables[h]
            q = rq_ref[pl.ds(r0, chunk), h * RET_QK:(h + 1) * RET_QK]
            k = rk_ref[pl.ds(r0, chunk), h * RET_QK:(h + 1) * RET_QK]
            v = rv_ref[pl.ds(r0, chunk), h * RET_V:(h + 1) * RET_V]
            state = s_ref[h]
            a = _dot_nt(q, k) * decay
            o = _dot(a.astype(BF16), v) + _dot(q, state.astype(BF16)) * cross
            kd = (k.astype(F32) * end).astype(BF16)
            s_ref[h] = total * state + lax.dot_general(
                kd, v, (((0,), (0,)), ((), ())), preferred_element_type=F32)
            y_ref[pl.ds(r0, chunk), h * RET_V:(h + 1) * RET_V] = _group_norm(o)
        return carry

    lax.fori_loop(0, n_chunks, body, 0)

    @pl.when(si == pl.num_programs(1) - 1)
    def _():
        st_ref[...] = s_ref[...]


def _ret_prompt(lgtab, rq, rk, rv, batch, seq, chunk, ts):
    ns = seq // ts
    row = lambda width: pl.BlockSpec((ts, width), lambda b, s: (b * ns + s, 0))
    return pl.pallas_call(
        functools.partial(_ret_prompt_kernel, chunk, ts // chunk),
        grid=(batch, ns),
        in_specs=[_const_spec((RET_HEADS, 8, LANES)), row(RET_HEADS * RET_QK),
                  row(RET_HEADS * RET_QK), row(RET_HEADS * RET_V)],
        out_specs=[row(RET_HEADS * RET_V),
                   pl.BlockSpec((None, RET_HEADS, RET_QK, RET_V), lambda b, s: (b, 0, 0, 0))],
        out_shape=[jax.ShapeDtypeStruct((batch * seq, RET_HEADS * RET_V), F32),
                   jax.ShapeDtypeStruct((batch, RET_HEADS, RET_QK, RET_V), F32)],
        scratch_shapes=[pltpu.VMEM((RET_HEADS, RET_QK, RET_V), F32)],
        compiler_params=pltpu.CompilerParams(dimension_semantics=("parallel", "arbitrary"),
                                             vmem_limit_bytes=VMEM_LIMIT),
        name="ret_prompt",
    )(lgtab, rq, rk, rv)


def _ret_sample_kernel(bb, t, lg_ref, rq_ref, rk_ref, rv_ref, st_ref, stack_hbm, y_ref, stn_ref):
    del stack_hbm
    for h in range(RET_HEADS):
        lg = lg_ref[h][0:1, 0:1]
        decay, cross, end, total = _decay_tables(lg, t)
        for bi in range(bb):
            rows = slice(bi * t, (bi + 1) * t)
            q = rq_ref[rows, h * RET_QK:(h + 1) * RET_QK].astype(BF16)
            k = rk_ref[rows, h * RET_QK:(h + 1) * RET_QK]
            v = rv_ref[rows, h * RET_V:(h + 1) * RET_V].astype(BF16).astype(F32)
            state = st_ref[bi, h]
            a = (_dot_nt(q, k.astype(BF16)) * decay).astype(BF16).astype(F32)
            o = _dot(q, state.astype(BF16)) * cross
            for j in range(t):
                o = o + a[:, j:j + 1] * v[j:j + 1, :]
            y_ref[rows, h * RET_V:(h + 1) * RET_V] = _group_norm(o)
            kd = (k * end).astype(BF16).astype(F32)
            stn_ref[bi, h] = total * state + _dot(kd.T, v)


def _ret_sample(layer, lgtab, rq, rk, rv, state, stacked, bb):
    dec_batch = state.shape[1]
    t = rq.shape[0] // dec_batch
    row = lambda width: pl.BlockSpec((bb * t, width), lambda i: (i, 0))
    st_spec = pl.BlockSpec((None, bb, RET_HEADS, RET_QK, RET_V), lambda i: (layer, i, 0, 0, 0))
    return pl.pallas_call(
        functools.partial(_ret_sample_kernel, bb, t),
        grid=(dec_batch // bb,),
        in_specs=[_const_spec((RET_HEADS, 8, LANES)), row(RET_HEADS * RET_QK),
                  row(RET_HEADS * RET_QK), row(RET_HEADS * RET_V), st_spec,
                  pl.BlockSpec(memory_space=pl.ANY)],
        out_specs=[row(RET_HEADS * RET_V), st_spec],
        out_shape=[jax.ShapeDtypeStruct((dec_batch * t, RET_HEADS * RET_V), F32),
                   jax.ShapeDtypeStruct(state.shape, F32)],
        input_output_aliases={5: 1},
        compiler_params=pltpu.CompilerParams(dimension_semantics=("parallel",),
                                             vmem_limit_bytes=VMEM_LIMIT),
        name="ret_sample",
    )(lgtab, rq, rk, rv, state, stacked)


def _sattn_kernel(layer, n_pages, chunk_pages, t, pt_ref, q_ref, qabs_ref, cnew_ref, krnew_ref,
                  wkt_ref, wv_ref, cc_hbm, ckrt_hbm, o_ref, cbuf, krbuf, sem):
    b = pl.program_id(0)
    nb = pl.num_programs(0)
    slots = cbuf.shape[0]
    slot = lax.rem(b, slots)
    page = cbuf.shape[2]
    hq = MLA_HEADS * t

    def start_fetch(bb, sl, lo, hi):
        for p in range(lo, hi):
            pg = pt_ref[bb, p]
            pltpu.make_async_copy(cc_hbm.at[layer, pg], cbuf.at[sl, p],
                                  sem.at[0, sl]).start(priority=p % 2)
            pltpu.make_async_copy(ckrt_hbm.at[layer, pg], krbuf.at[sl, p],
                                  sem.at[1, sl]).start(priority=(p + 1) % 2)

    def wait_fetch(sl):
        pages = pl.ds(0, n_pages)
        pltpu.make_async_copy(cc_hbm.at[layer, pages], cbuf.at[sl], sem.at[0, sl]).wait()
        pltpu.make_async_copy(ckrt_hbm.at[layer, pages], krbuf.at[sl], sem.at[1, sl]).wait()

    @pl.when(b == 0)
    def _():
        start_fetch(0, 0, 0, n_pages)
        start_fetch(lax.rem(1, nb), 1, 0, n_pages)

    wait_fetch(slot)
    b_ahead = lax.rem(b + 2, nb)
    slot_ahead = lax.rem(b + 2, slots)

    qh = q_ref[...]
    qr = jnp.concatenate(
        [qh[:, h * LANES + ROPE_LO:h * LANES + ROPE_LO + QK_ROPE] for h in range(MLA_HEADS)],
        axis=0).astype(BF16)
    qa = jnp.concatenate(
        [qabs_ref[:, h * KV_LORA:(h + 1) * KV_LORA] for h in range(MLA_HEADS)],
        axis=0).astype(BF16)
    lhs = jnp.concatenate([wkt_ref[...], qa], axis=0)
    n_exp = MLA_HEADS * QK_NOPE

    def scores(c, s_rope, mask):
        n = c.shape[0]
        cb = c.astype(BF16)
        big = _dot_nt(lhs, cb)
        kt = big[:n_exp].reshape(MLA_HEADS, QK_NOPE, n)
        inv = lax.rsqrt(jnp.mean(kt * kt, axis=1, keepdims=True) + EPS)
        s = big[n_exp:].reshape(MLA_HEADS, t, n) * inv + s_rope.reshape(MLA_HEADS, t, n)
        if mask is not None:
            s = jnp.where(mask, s, -jnp.inf)
        return s, cb

    def update(carry, s, cb):
        m, l, acc = carry
        n = s.shape[-1]
        m_new = jnp.maximum(m, jnp.max(s, axis=-1, keepdims=True))
        alpha = jnp.exp2(m - m_new)
        p = jnp.exp2(s - m_new)
        l = alpha * l + jnp.sum(p, axis=-1, keepdims=True)
        acc = alpha.reshape(hq, 1) * acc + _dot(p.reshape(hq, n).astype(BF16), cb)
        return m_new, l, acc

    carry = (jnp.full((MLA_HEADS, t, 1), -jnp.inf, F32), jnp.zeros((MLA_HEADS, t, 1), F32),
             jnp.zeros((hq, KV_LORA), F32))
    n_keys = chunk_pages * page
    pending = None
    for j in range(n_pages // chunk_pages):
        c = cbuf[slot, pl.ds(j * chunk_pages, chunk_pages)].reshape(n_keys, KV_LORA)
        krt = jnp.concatenate(
            [krbuf[slot, j * chunk_pages + i] for i in range(chunk_pages)], axis=1)
        nxt = scores(c, _dot(qr, krt.astype(BF16)), None)
        start_fetch(b_ahead, slot_ahead, j * chunk_pages, (j + 1) * chunk_pages)
        if pending is not None:
            carry = update(carry, *pending)
        pending = nxt

    pad = LANES - t
    c = jnp.concatenate([cnew_ref[...], jnp.zeros((pad, KV_LORA), F32)], axis=0)
    kr = jnp.concatenate([krnew_ref[...], jnp.zeros((pad, QK_ROPE), F32)], axis=0)
    shape = (MLA_HEADS, t, LANES)
    mask = lax.broadcasted_iota(jnp.int32, shape, 2) <= lax.broadcasted_iota(jnp.int32, shape, 1)
    last = scores(c, _dot_nt(qr, kr.astype(BF16)), mask)
    carry = update(carry, *pending)
    m, l, acc = update(carry, *last)

    @pl.when(b == nb - 1)
    def _():
        wait_fetch(lax.rem(b + 1, slots))
        wait_fetch(slot_ahead)

    oc = (acc / l.reshape(hq, 1)).astype(BF16)
    full = _dot(oc, wv_ref[...])
    for h in range(MLA_HEADS):
        o_ref[:, h * V_HEAD:(h + 1) * V_HEAD] = full[h * t:(h + 1) * t, h * V_HEAD:(h + 1) * V_HEAD]


def _mla_sample(layer, page_table, q, qabs, cnew, krnew, wkt, wv, cache_ckv, cache_krope_t,
                chunk_pages):
    dec_batch, n_pages = page_table.shape
    t = q.shape[0] // dec_batch
    page = cache_ckv.shape[2]
    row = lambda width: pl.BlockSpec((t, width), lambda b, pt: (b, 0))
    const = functools.partial(_layer_spec, layer)
    grid_spec = pltpu.PrefetchScalarGridSpec(
        num_scalar_prefetch=1, grid=(dec_batch,),
        in_specs=[row(MLA_HEADS * LANES), row(MLA_HEADS * KV_LORA), row(KV_LORA), row(QK_ROPE),
                  const((MLA_HEADS * QK_NOPE, KV_LORA)), const((KV_LORA, MLA_HEADS * V_HEAD)),
                  pl.BlockSpec(memory_space=pl.ANY), pl.BlockSpec(memory_space=pl.ANY)],
        out_specs=row(MLA_HEADS * V_HEAD),
        scratch_shapes=[pltpu.VMEM((SAMPLE_SLOTS, n_pages, page, KV_LORA), F32),
                        pltpu.VMEM((SAMPLE_SLOTS, n_pages, QK_ROPE, page), F32),
                        pltpu.SemaphoreType.DMA((2, SAMPLE_SLOTS))])
    return pl.pallas_call(
        functools.partial(_sattn_kernel, layer, n_pages, chunk_pages, t),
        grid_spec=grid_spec,
        out_shape=jax.ShapeDtypeStruct((dec_batch * t, MLA_HEADS * V_HEAD), F32),
        compiler_params=pltpu.CompilerParams(dimension_semantics=("arbitrary",),
                                             vmem_limit_bytes=VMEM_LIMIT),
        name="mla_sample",
    )(page_table, q, qabs, cnew, krnew, wkt, wv, cache_ckv, cache_krope_t)


def _out_kernel(x_ref, om_ref, y_ref, rg_ref, ga_ref, gb_ref, gret_ref, wo_ref, ln2_ref,
                wup_ref, wdn_ref, o_ref):
    rg = rg_ref[...].astype(F32)
    o_ret = y_ref[...] * gret_ref[...] * (rg * _sigmoid(rg))
    mixed = (_sigmoid(ga_ref[...].astype(F32)) * om_ref[...].astype(F32)
             + _sigmoid(gb_ref[...].astype(F32)) * o_ret)
    x1 = x_ref[...] + _dot(mixed.astype(BF16), wo_ref[...])
    h2 = _rmsnorm(x1, ln2_ref[...]).astype(BF16)
    u = jnp.maximum(_dot(h2, wup_ref[...]), 0.0)
    o_ref[...] = x1 + _dot((u * u).astype(BF16), wdn_ref[...])


def _out_proj(layer, x, om, y, rg, ga, gb, w, tm):
    T = x.shape[0]
    lspec = functools.partial(_layer_spec, layer)
    row = pl.BlockSpec((tm, D_MODEL), lambda i: (i, 0))
    return pl.pallas_call(
        _out_kernel,
        grid=(T // tm,),
        in_specs=[row, row, row, row, row, row, lspec((1, D_MODEL)),
                  lspec((D_MODEL, D_MODEL)), lspec((1, D_MODEL)),
                  lspec((D_MODEL, D_FF)), lspec((D_FF, D_MODEL))],
        out_specs=row,
        out_shape=jax.ShapeDtypeStruct((T, D_MODEL), F32),
        compiler_params=pltpu.CompilerParams(dimension_semantics=("parallel",),
                                             vmem_limit_bytes=VMEM_LIMIT),
        name="out_proj",
    )(x, om, y, rg, ga, gb, w["g_ret"], w["w_o"], w["ln2"], w["w_up"], w["w_down"])


def _prep_weights(ln1, w_in, g_qlat, w_uq, g_qn, g_qr, g_kvlat, g_kr, w_ukv, g_kn, g_ret,
                  w_o, ln2, w_up, w_down):
    depth = w_in.shape[0]
    lat = Q_LORA + KV_LORA
    qk = QK_NOPE + QK_ROPE
    zeros = lambda *shape: jnp.zeros((depth,) + shape, F32)
    w_in_p = jnp.concatenate(
        [w_in[..., :lat], w_in[..., lat + QK_ROPE:], zeros(D_MODEL, ROPE_LO),
         w_in[..., lat:lat + QK_ROPE], zeros(D_MODEL, LANES - ROPE_LO - QK_ROPE)], axis=-1)
    w_uq_p = jnp.pad(w_uq.reshape(depth, Q_LORA, MLA_HEADS, qk),
                     ((0, 0), (0, 0), (0, 0), (0, LANES - qk)))
    kv = w_ukv.reshape(depth, KV_LORA, MLA_HEADS, QK_NOPE + V_HEAD)
    wk, wv = kv[..., :QK_NOPE], kv[..., QK_NOPE:]
    w_uk_p = jnp.pad(wk, ((0, 0), (0, 0), (0, 0), (0, LANES - QK_NOPE)))
    wkt = wk.transpose(0, 2, 3, 1)
    scale = float(qk ** -0.5 * np.log2(np.e))
    vec = lambda v: v.reshape(depth, 1, -1)
    return {
        "ln1": vec(ln1), "w_in": w_in_p.astype(BF16), "g_qlat": vec(g_qlat),
        "w_uq": w_uq_p.reshape(depth, Q_LORA, MLA_HEADS * LANES).astype(BF16),
        "g_q": vec(jnp.concatenate([g_qn, g_qr, zeros(LANES - qk)], axis=-1) * scale),
        "g_kvlat": vec(g_kvlat),
        "g_kr": vec(jnp.concatenate([zeros(ROPE_LO), g_kr, zeros(LANES - ROPE_LO - QK_ROPE)],
                                    axis=-1)),
        "g_k": vec(jnp.concatenate([g_kn, zeros(LANES - QK_NOPE)], axis=-1)),
        "w_uk": w_uk_p.reshape(depth, KV_LORA, MLA_HEADS * LANES).astype(BF16),
        "w_uvt": wv.transpose(0, 2, 3, 1).reshape(depth, MLA_HEADS * V_HEAD, KV_LORA).astype(BF16),
        "w_ukt": jnp.pad(wkt, ((0, 0), (0, 0), (0, LANES - QK_NOPE), (0, 0))).reshape(
            depth, MLA_HEADS * LANES, KV_LORA).astype(BF16),
        "w_ukt_all": wkt.reshape(depth, MLA_HEADS * QK_NOPE, KV_LORA).astype(BF16),
        "w_uv": wv.reshape(depth, KV_LORA, MLA_HEADS * V_HEAD).astype(BF16),
        "gmat": jnp.asarray(_group_mean_matrix(), BF16),
        "g_ret": vec(g_ret), "w_o": w_o.astype(BF16), "ln2": vec(ln2),
        "w_up": w_up.astype(BF16), "w_down": w_down.astype(BF16),
    }


def _rope_tables(pos):
    def cos_sin(dim):
        inv = ROPE_THETA ** (-jnp.arange(0, dim, 2, dtype=F32) / dim)
        ang = pos.astype(F32)[:, None] * inv[None, :]
        return jnp.cos(ang), jnp.sin(ang)
    n = pos.shape[0]
    c_r, s_r = cos_sin(RET_QK)
    c_m, s_m = cos_sin(QK_ROPE)
    tail = jnp.zeros((n, LANES - ROPE_LO - QK_ROPE), F32)
    return {
        "cr": jnp.concatenate([c_r, c_r], axis=1),
        "sr": jnp.concatenate([-s_r, s_r], axis=1),
        "cm": jnp.concatenate([jnp.ones((n, ROPE_LO), F32), c_m, c_m, tail], axis=1),
        "sm": jnp.concatenate([jnp.zeros((n, ROPE_LO), F32), -s_m, s_m, tail], axis=1),
    }


def kernel(x_prompt, x_sample, cache_ckv, cache_krope, state_ret, page_table, ln1, w_in, g_qlat,
           w_uq, g_qn, g_qr, g_kvlat, g_kr, w_ukv, g_kn, g_ret, w_o, ln2, w_up, w_down):
    batch, seq, _ = x_prompt.shape
    dec_batch, dec_seq, _ = x_sample.shape
    depth = w_in.shape[0]
    n_pages = page_table.shape[1]
    past = n_pages * cache_ckv.shape[2]

    tm_p = min(256, seq)
    tm_s = min(256, dec_batch * dec_seq)
    tq = min(512, seq)
    ret_chunk = min(128, seq)
    ret_ts = min(1024, seq)
    chunk_pages = min(16, n_pages)
    bb = min(4, dec_batch)
    cache_krope_t = jnp.swapaxes(cache_krope, 2, 3)
    st_sample = jnp.zeros(state_ret.shape, F32)

    tabs_p = _rope_tables(jnp.arange(seq))
    tabs_s = {k: jnp.tile(v, (tm_s // dec_seq, 1))
              for k, v in _rope_tables(past + jnp.arange(dec_seq)).items()}
    lgtab = jnp.broadcast_to(
        jnp.log1p(-jnp.exp2(-5.0 - jnp.arange(RET_HEADS, dtype=F32)))[:, None, None],
        (RET_HEADS, 8, LANES))

    xp = x_prompt.reshape(batch * seq, D_MODEL)
    xs = x_sample.reshape(dec_batch * dec_seq, D_MODEL)
    outs = [[] for _ in range(5)]
    w = _prep_weights(ln1, w_in, g_qlat, w_uq, g_qn, g_qr, g_kvlat, g_kr, w_ukv, g_kn, g_ret,
                      w_o, ln2, w_up, w_down)
    for l in range(depth):
        q, ckv, kr, rq, rk, rv, rg, ga, gb, k, vt = _in_proj(
            "prompt", l, xp, w, tabs_p, seq // tm_p, tm_p, batch, seq)
        o_mla = _mla_prompt(q, k, vt, batch, seq, tq, 4)
        y, st = _ret_prompt(lgtab, rq, rk, rv, batch, seq, ret_chunk, ret_ts)
        xp = _out_proj(l, xp, o_mla, y, rg, ga, gb, w, tm_p)
        outs[0].append(ckv.reshape(batch, seq, KV_LORA))
        outs[1].append(kr.reshape(batch, seq, QK_ROPE))
        outs[2].append(st)

        q, ckv, kr, rq, rk, rv, rg, ga, gb, qabs = _in_proj(
            "sample", l, xs, w, tabs_s, 1, tm_s, dec_batch, dec_seq)
        o_mla = _mla_sample(l, page_table, q, qabs, ckv, kr, w["w_ukt_all"], w["w_uv"],
                            cache_ckv, cache_krope_t, chunk_pages)
        y, st_sample = _ret_sample(l, lgtab, rq, rk, rv, state_ret, st_sample, bb)
        xs = _out_proj(l, xs, o_mla, y, rg, ga, gb, w, tm_s)
        outs[3].append(ckv.reshape(dec_batch, dec_seq, KV_LORA))
        outs[4].append(kr.reshape(dec_batch, dec_seq, QK_ROPE))

    return (xp.reshape(batch, seq, D_MODEL), xs.reshape(dec_batch, dec_seq, D_MODEL),
            jnp.stack(outs[0]), jnp.stack(outs[1]), jnp.stack(outs[2]),
            jnp.stack(outs[3]), jnp.stack(outs[4]), st_sample)
```

```python
import functools

import numpy as np
import jax
import jax.numpy as jnp
from jax import lax
from jax.experimental import pallas as pl
from jax.experimental.pallas import tpu as pltpu

F32 = jnp.float32
BF16 = jnp.bfloat16

D_MODEL = 1024
MLA_HEADS = 8
V_HEAD = 128
Q_LORA = 384
KV_LORA = 256
QK_NOPE = 64
QK_ROPE = 32
RET_HEADS = 4
RET_V = 256
RET_QK = 128
D_FF = 4096
ROPE_THETA = 10000.0
EPS = 1e-6

LANES = 128
VMEM_LIMIT = 52 << 20
SAMPLE_SLOTS = 4

_SEG = {}
_off = 0
for _name, _w in (("qlat", Q_LORA), ("kvlat", KV_LORA), ("rq", RET_HEADS * RET_QK),
                  ("rk", RET_HEADS * RET_QK), ("rv", RET_HEADS * RET_V), ("rg", D_MODEL),
                  ("ga", D_MODEL), ("gb", D_MODEL), ("kr", LANES)):
    _SEG[_name] = (_off, _off + _w)
    _off += _w
D_IN_P = _off
ROPE_LO = QK_NOPE
ROPE_HALF = QK_ROPE // 2


def _group_mean_matrix():
    gid = np.array([0] * QK_NOPE + [1] * QK_ROPE + [2] * (LANES - QK_NOPE - QK_ROPE))
    size = np.array([QK_NOPE, QK_ROPE, LANES - QK_NOPE - QK_ROPE], np.float32)
    g = (gid[:, None] == gid[None, :]).astype(np.float32) / size[gid][None, :]
    out = np.zeros((2 * LANES, 2 * LANES), np.float32)
    out[:LANES, :LANES] = g
    out[LANES:, LANES:] = g
    return out


def _const_spec(shape):
    nd = len(shape)
    return pl.BlockSpec(shape, lambda *_: (0,) * nd, pipeline_mode=pl.Buffered(1))


def _layer_spec(layer, shape):
    nd = len(shape)
    return pl.BlockSpec((None,) + tuple(shape), lambda *_: (layer,) + (0,) * nd,
                        pipeline_mode=pl.Buffered(1))


def _rmsnorm(x, g):
    return x * lax.rsqrt(jnp.mean(x * x, axis=-1, keepdims=True) + EPS) * g


def _sigmoid(x):
    return 0.5 * jnp.tanh(0.5 * x) + 0.5


def _dot(a, b):
    return jnp.dot(a, b, preferred_element_type=F32)


def _dot_nt(a, b):
    return lax.dot_general(a, b, (((1,), (1,)), ((), ())), preferred_element_type=F32)


def _head_norm(v, g_ref, gain2):
    heads = []
    for p in range(MLA_HEADS // 2):
        vp = v[:, p * 2 * LANES:(p + 1) * 2 * LANES]
        ms = _dot((vp * vp).astype(BF16), g_ref[...])
        n = vp * lax.rsqrt(ms + EPS) * gain2
        heads.append(n[:, :LANES])
        heads.append(n[:, LANES:])
    return heads


def _rope32(n, cm, sm):
    lane = lax.broadcasted_iota(jnp.int32, n.shape, 1)
    rot = jnp.where(lane < ROPE_LO + ROPE_HALF,
                    pltpu.roll(n, LANES - ROPE_HALF, axis=1),
                    pltpu.roll(n, ROPE_HALF, axis=1))
    return n * cm + rot * sm


def _in_kernel(mode, x_ref, ln1_ref, win_ref, gql_ref, wuq_ref, gq_ref, gkvl_ref, gkr_ref,
               g_ref, cr_ref, sr_ref, cm_ref, sm_ref, wk_ref, gk_ref, *rest):
    if mode == "prompt":
        wuvt_ref = rest[0]
        q_o, ckv_o, kr_o, rq_o, rk_o, rv_o, rg_o, ga_o, gb_o, k_o, vt_o = rest[1:]
    else:
        q_o, ckv_o, kr_o, rq_o, rk_o, rv_o, rg_o, ga_o, gb_o, qabs_o = rest

    h = _rmsnorm(x_ref[...], ln1_ref[...]).astype(BF16)

    def seg(name):
        a, b = _SEG[name]
        return _dot(h, win_ref[:, a:b])

    cm = cm_ref[...]
    sm = sm_ref[...]
    cr = cr_ref[...]
    sr = sr_ref[...]

    z_qlat = seg("qlat")
    z_kv = seg("kvlat")
    z_kr = seg("kr")
    rq = seg("rq")
    rk = seg("rk")

    qn = _rmsnorm(z_qlat, gql_ref[...]).astype(BF16)
    q = _dot(qn, wuq_ref[...])
    rv_o[...] = seg("rv").astype(rv_o.dtype)
    gq2 = jnp.concatenate([gq_ref[...], gq_ref[...]], axis=1)
    q_heads = [_rope32(n, cm, sm) for n in _head_norm(q, g_ref, gq2)]
    for hh in range(MLA_HEADS):
        q_o[:, hh * LANES:(hh + 1) * LANES] = q_heads[hh].astype(q_o.dtype)
    z_rg = seg("rg")
    rg_o[...] = (z_rg * _sigmoid(z_rg)).astype(rg_o.dtype)

    c = _rmsnorm(z_kv, gkvl_ref[...])
    ckv_o[...] = c
    cb = c.astype(BF16)
    if mode == "prompt":
        kexp = _dot(cb, wk_ref[...])
        vt = _dot_nt(wuvt_ref[...], cb)
    ga_o[...] = _sigmoid(seg("ga")).astype(ga_o.dtype)
    ms = _dot((z_kr * z_kr).astype(BF16), g_ref[:LANES, :LANES])
    krr = _rope32(z_kr * lax.rsqrt(ms + EPS) * gkr_ref[...], cm, sm)
    kr_o[...] = krr[:, ROPE_LO:ROPE_LO + QK_ROPE]

    if mode == "prompt":
        gk2 = jnp.concatenate([gk_ref[...], gk_ref[...]], axis=1)
        k_heads = _head_norm(kexp, g_ref, gk2)
        for hh in range(MLA_HEADS):
            k_o[:, hh * LANES:(hh + 1) * LANES] = (k_heads[hh] + krr).astype(k_o.dtype)
        vt_o[...] = vt.astype(vt_o.dtype)
    else:
        gk = gk_ref[...]
        for hh in range(MLA_HEADS):
            qg = (q_heads[hh] * gk).astype(BF16)
            qabs_o[:, hh * KV_LORA:(hh + 1) * KV_LORA] = _dot(
                qg, wk_ref[hh * LANES:(hh + 1) * LANES, :]).astype(qabs_o.dtype)
    gb_o[...] = _sigmoid(seg("gb")).astype(gb_o.dtype)

    for hh in range(RET_HEADS):
        sl = slice(hh * RET_QK, (hh + 1) * RET_QK)
        a = rq[:, sl]
        rq_o[:, sl] = (a * cr + pltpu.roll(a, RET_QK // 2, axis=1) * sr).astype(rq_o.dtype)
        b = rk[:, sl]
        rk_o[:, sl] = ((b * cr + pltpu.roll(b, RET_QK // 2, axis=1) * sr)
                       * (RET_QK ** -0.5)).astype(rk_o.dtype)


def _in_proj(mode, layer, x, w, tabs, n_seq_blocks, tm, batch, seq):
    T = x.shape[0]
    lspec = functools.partial(_layer_spec, layer)
    nt = T // tm
    act = BF16 if mode == "prompt" else F32
    row = lambda width: pl.BlockSpec((tm, width), lambda i: (i, 0))
    tab = pl.BlockSpec((tm, LANES), lambda i: (i % n_seq_blocks, 0))
    in_specs = [row(D_MODEL), lspec((1, D_MODEL)), lspec((D_MODEL, D_IN_P)),
                lspec((1, Q_LORA)), lspec((Q_LORA, MLA_HEADS * LANES)),
                lspec((1, LANES)), lspec((1, KV_LORA)), lspec((1, LANES)),
                _const_spec((2 * LANES, 2 * LANES)), tab, tab, tab, tab]
    args = [x, w["ln1"], w["w_in"], w["g_qlat"], w["w_uq"], w["g_q"], w["g_kvlat"], w["g_kr"],
            w["gmat"], tabs["cr"], tabs["sr"], tabs["cm"], tabs["sm"]]
    out_shape = [jax.ShapeDtypeStruct((T, MLA_HEADS * LANES), act),
                 jax.ShapeDtypeStruct((T, KV_LORA), F32),
                 jax.ShapeDtypeStruct((T, QK_ROPE), F32),
                 jax.ShapeDtypeStruct((T, RET_HEADS * RET_QK), act),
                 jax.ShapeDtypeStruct((T, RET_HEADS * RET_QK), act),
                 jax.ShapeDtypeStruct((T, RET_HEADS * RET_V), act),
                 jax.ShapeDtypeStruct((T, D_MODEL), act),
                 jax.ShapeDtypeStruct((T, D_MODEL), act),
                 jax.ShapeDtypeStruct((T, D_MODEL), act)]
    out_specs = [row(MLA_HEADS * LANES), row(KV_LORA), row(QK_ROPE), row(RET_HEADS * RET_QK),
                 row(RET_HEADS * RET_QK), row(RET_HEADS * RET_V), row(D_MODEL), row(D_MODEL),
                 row(D_MODEL)]
    if mode == "prompt":
        in_specs += [lspec((KV_LORA, MLA_HEADS * LANES)), lspec((1, LANES)),
                     lspec((MLA_HEADS * V_HEAD, KV_LORA))]
        args += [w["w_uk"], w["g_k"], w["w_uvt"]]
        out_shape += [jax.ShapeDtypeStruct((T, MLA_HEADS * LANES), BF16),
                      jax.ShapeDtypeStruct((batch, MLA_HEADS * V_HEAD, seq), BF16)]
        out_specs += [row(MLA_HEADS * LANES),
                      pl.BlockSpec((None, MLA_HEADS * V_HEAD, tm),
                                   lambda i: (i // n_seq_blocks, 0, i % n_seq_blocks))]
    else:
        in_specs += [lspec((MLA_HEADS * LANES, KV_LORA)), lspec((1, LANES))]
        args += [w["w_ukt"], w["g_k"]]
        out_shape += [jax.ShapeDtypeStruct((T, MLA_HEADS * KV_LORA), F32)]
        out_specs += [row(MLA_HEADS * KV_LORA)]
    return pl.pallas_call(
        functools.partial(_in_kernel, mode),
        grid=(nt,), in_specs=in_specs, out_specs=out_specs, out_shape=out_shape,
        compiler_params=pltpu.CompilerParams(dimension_semantics=("parallel",),
                                             vmem_limit_bytes=VMEM_LIMIT),
        name="in_proj_" + mode,
    )(*args)


def _flash_kernel(t, nh, q_ref, k_ref, vt_ref, o_ref, s_ref, bm_ref, m_ref, l_ref, acc_ref):
    qi = pl.program_id(2)
    m_ref[...] = jnp.full(m_ref.shape, -jnp.inf, F32)
    l_ref[...] = jnp.zeros(l_ref.shape, F32)
    acc_ref[...] = jnp.zeros(acc_ref.shape, F32)

    def step(cur, nxt, masked):
        if masked:
            kpos = nxt * t + lax.broadcasted_iota(jnp.int32, (t, t), 0)
            qpos = qi * t + lax.broadcasted_iota(jnp.int32, (t, t), 1)
            keep = kpos <= qpos
        for hh in range(nh):
            hs = slice(hh * LANES, (hh + 1) * LANES)
            if cur is not None:
                s_cur = s_ref[hh]
                m_prev = m_ref[hh]
                m_new = jnp.maximum(m_prev, bm_ref[hh])
                alpha = jnp.exp2(m_prev - m_new)
            if nxt is not None:
                n0 = pl.multiple_of(nxt * t, t)
                s = _dot_nt(k_ref[pl.ds(n0, t), hs], q_ref[:, hs])
                if masked:
                    s = jnp.where(keep, s, -jnp.inf)
                s_ref[hh] = s
                bm_ref[hh] = jnp.max(s, axis=0, keepdims=True)
            if cur is not None:
                c0 = pl.multiple_of(cur * t, t)
                p = jnp.exp2(s_cur - m_new)
                l_ref[hh] = alpha * l_ref[hh] + jnp.sum(p, axis=0, keepdims=True)
                acc_ref[hh] = alpha * acc_ref[hh] + _dot(vt_ref[hs, pl.ds(c0, t)], p.astype(BF16))
                m_ref[hh] = m_new

    @pl.when(qi == 0)
    def _():
        step(None, 0, True)

    @pl.when(qi > 0)
    def _():
        step(None, 0, False)

        def body(i, carry):
            step(i, i + 1, False)
            return carry

        lax.fori_loop(0, qi - 1, body, 0)
        step(qi - 1, qi, True)

    step(qi, None, False)
    for hh in range(nh):
        o_ref[:, hh * V_HEAD:(hh + 1) * V_HEAD] = (acc_ref[hh] / l_ref[hh]).T.astype(o_ref.dtype)


def _mla_prompt(q, k, vt, batch, seq, t, nh):
    nq = seq // t
    stat = pltpu.VMEM((nh, 1, t), F32)
    return pl.pallas_call(
        functools.partial(_flash_kernel, t, nh),
        grid=(batch, MLA_HEADS // nh, nq),
        in_specs=[pl.BlockSpec((t, nh * LANES), lambda b, h, i: (b * nq + i, h)),
                  pl.BlockSpec((seq, nh * LANES), lambda b, h, i: (b, h)),
                  pl.BlockSpec((None, nh * V_HEAD, seq), lambda b, h, i: (b, h, 0))],
        out_specs=pl.BlockSpec((t, nh * V_HEAD), lambda b, h, i: (b * nq + i, h)),
        out_shape=jax.ShapeDtypeStruct((batch * seq, MLA_HEADS * V_HEAD), BF16),
        scratch_shapes=[pltpu.VMEM((nh, t, t), F32), stat, stat, stat,
                        pltpu.VMEM((nh, V_HEAD, t), F32)],
        compiler_params=pltpu.CompilerParams(
            dimension_semantics=("parallel", "parallel", "arbitrary"),
            vmem_limit_bytes=VMEM_LIMIT),
        name="mla_prompt",
    )(q, k, vt)


def _group_norm(o):
    mu = jnp.mean(o, axis=-1, keepdims=True)
    d = o - mu
    return d * lax.rsqrt(jnp.mean(d * d, axis=-1, keepdims=True) + EPS)


def _decay_tables(lg, c):
    ti = lax.broadcasted_iota(jnp.int32, (c, c), 0)
    tj = lax.broadcasted_iota(jnp.int32, (c, c), 1)
    diff = (ti - tj).astype(F32)
    decay = jnp.where(diff >= 0, jnp.exp(lg * jnp.maximum(diff, 0.0)), 0.0)
    t = lax.broadcasted_iota(jnp.int32, (c, 1), 0).astype(F32)
    cross = jnp.exp(lg * (t + 1.0))
    end = jnp.exp(lg * (c - 1.0 - t))
    total = jnp.exp(lg * float(c))
    return decay, cross, end, total


def _ret_prompt_kernel(chunk, n_chunks, lg_ref, rq_ref, rk_ref, rv_ref, y_ref, st_ref, s_ref):
    si = pl.program_id(1)
    tables = [_decay_tables(lg_ref[h][0:1, 0:1], chunk) for h in range(RET_HEADS)]

    @pl.when(si == 0)
    def _():
        s_ref[...] = jnp.zeros(s_ref.shape, F32)

    def body(ci, carry):
        r0 = pl.multiple_of(ci * chunk, chunk)
        first = []
        for h in range(RET_HEADS):
            end = tables[h][2]
            q = rq_ref[pl.ds(r0, chunk), h * RET_QK:(h + 1) * RET_QK]
            k = rk_ref[pl.ds(r0, chunk), h * RET_QK:(h + 1) * RET_QK]
            v = rv_ref[pl.ds(r0, chunk), h * RET_V:(h + 1) * RET_V]
            state = s_ref[h]
            kd = (k.astype(F32) * end).astype(BF16)
            first.append((_dot_nt(q, k), _dot(q, state.astype(BF16)),
                          lax.dot_general(kd, v, (((0,), (0,)), ((), ())),
                                          preferred_element_type=F32), v, state))
        for h in range(RET_HEADS):
            decay, cross, _, total = tables[h]
            qk, qs, kv, v, state = first[h]
            o = _dot((qk * decay).astype(BF16), v) + qs * cross
            s_ref[h] = total * state + kv
            y_ref[pl.ds(r0, chunk), h * RET_V:(h + 1) * RET_V] = _group_norm(o)
        return carry

    lax.fori_loop(0, n_chunks, body, 0)

    @pl.when(si == pl.num_programs(1) - 1)
    def _():
        st_ref[...] = s_ref[...]


def _ret_prompt(lgtab, rq, rk, rv, batch, seq, chunk, ts):
    ns = seq // ts
    row = lambda width: pl.BlockSpec((ts, width), lambda b, s: (b * ns + s, 0))
    return pl.pallas_call(
        functools.partial(_ret_prompt_kernel, chunk, ts // chunk),
        grid=(batch, ns),
        in_specs=[_const_spec((RET_HEADS, 8, LANES)), row(RET_HEADS * RET_QK),
                  row(RET_HEADS * RET_QK), row(RET_HEADS * RET_V)],
        out_specs=[row(RET_HEADS * RET_V),
                   pl.BlockSpec((None, RET_HEADS, RET_QK, RET_V), lambda b, s: (b, 0, 0, 0))],
        out_shape=[jax.ShapeDtypeStruct((batch * seq, RET_HEADS * RET_V), F32),
                   jax.ShapeDtypeStruct((batch, RET_HEADS, RET_QK, RET_V), F32)],
        scratch_shapes=[pltpu.VMEM((RET_HEADS, RET_QK, RET_V), F32)],
        compiler_params=pltpu.CompilerParams(dimension_semantics=("parallel", "arbitrary"),
                                             vmem_limit_bytes=VMEM_LIMIT),
        name="ret_prompt",
    )(lgtab, rq, rk, rv)


def _ret_sample_kernel(bb, t, lg_ref, rq_ref, rk_ref, rv_ref, st_ref, stack_hbm, y_ref, stn_ref):
    del stack_hbm
    tables = [_decay_tables(lg_ref[h][0:1, 0:1], t) for h in range(RET_HEADS)]
    pairs = [(h, bi) for h in range(RET_HEADS) for bi in range(bb)]
    first = []
    for h, bi in pairs:
        end = tables[h][2]
        rows = slice(bi * t, (bi + 1) * t)
        q = rq_ref[rows, h * RET_QK:(h + 1) * RET_QK].astype(BF16)
        k = rk_ref[rows, h * RET_QK:(h + 1) * RET_QK]
        v = rv_ref[rows, h * RET_V:(h + 1) * RET_V].astype(BF16).astype(F32)
        state = st_ref[bi, h]
        kd = (k * end).astype(BF16).astype(F32)
        first.append((_dot_nt(q, k.astype(BF16)), _dot(q, state.astype(BF16)), _dot(kd.T, v),
                      v, state))
    for (h, bi), (qk, qs, kv, v, state) in zip(pairs, first):
        decay, cross, _, total = tables[h]
        rows = slice(bi * t, (bi + 1) * t)
        a = (qk * decay).astype(BF16).astype(F32)
        o = qs * cross
        for j in range(t):
            o = o + a[:, j:j + 1] * v[j:j + 1, :]
        y_ref[rows, h * RET_V:(h + 1) * RET_V] = _group_norm(o)
        stn_ref[bi, h] = total * state + kv


def _ret_sample(layer, lgtab, rq, rk, rv, state, stacked, bb):
    dec_batch = state.shape[1]
    t = rq.shape[0] // dec_batch
    row = lambda width: pl.BlockSpec((bb * t, width), lambda i: (i, 0))
    st_spec = pl.BlockSpec((None, bb, RET_HEADS, RET_QK, RET_V), lambda i: (layer, i, 0, 0, 0))
    return pl.pallas_call(
        functools.partial(_ret_sample_kernel, bb, t),
        grid=(dec_batch // bb,),
        in_specs=[_const_spec((RET_HEADS, 8, LANES)), row(RET_HEADS * RET_QK),
                  row(RET_HEADS * RET_QK), row(RET_HEADS * RET_V), st_spec,
                  pl.BlockSpec(memory_space=pl.ANY)],
        out_specs=[row(RET_HEADS * RET_V), st_spec],
        out_shape=[jax.ShapeDtypeStruct((dec_batch * t, RET_HEADS * RET_V), F32),
                   jax.ShapeDtypeStruct(state.shape, F32)],
        input_output_aliases={5: 1},
        compiler_params=pltpu.CompilerParams(dimension_semantics=("parallel",),
                                             vmem_limit_bytes=VMEM_LIMIT),
        name="ret_sample",
    )(lgtab, rq, rk, rv, state, stacked)


def _sattn_kernel(layer, n_pages, chunk_pages, t, pt_ref, q_ref, qabs_ref, cnew_ref, krnew_ref,
                  wkt_ref, wv_ref, cc_hbm, ckrt_hbm, o_ref, cbuf, krbuf, sem):
    b = pl.program_id(0)
    nb = pl.num_programs(0)
    slots = cbuf.shape[0]
    slot = lax.rem(b, slots)
    page = cbuf.shape[2]
    hq = MLA_HEADS * t

    def start_fetch(bb, sl, lo, hi):
        for p in range(lo, hi):
            pg = pt_ref[bb, p]
            pltpu.make_async_copy(cc_hbm.at[layer, pg], cbuf.at[sl, p],
                                  sem.at[0, sl]).start(priority=p % 2)
            pltpu.make_async_copy(ckrt_hbm.at[layer, pg], krbuf.at[sl, p],
                                  sem.at[1, sl]).start(priority=(p + 1) % 2)

    def wait_fetch(sl):
        pages = pl.ds(0, n_pages)
        pltpu.make_async_copy(cc_hbm.at[layer, pages], cbuf.at[sl], sem.at[0, sl]).wait()
        pltpu.make_async_copy(ckrt_hbm.at[layer, pages], krbuf.at[sl], sem.at[1, sl]).wait()

    @pl.when(b == 0)
    def _():
        start_fetch(0, 0, 0, n_pages)
        start_fetch(lax.rem(1, nb), 1, 0, n_pages)

    wait_fetch(slot)
    b_ahead = lax.rem(b + 2, nb)
    slot_ahead = lax.rem(b + 2, slots)

    qh = q_ref[...]
    qr = jnp.concatenate(
        [qh[:, h * LANES + ROPE_LO:h * LANES + ROPE_LO + QK_ROPE] for h in range(MLA_HEADS)],
        axis=0).astype(BF16)
    qa = jnp.concatenate(
        [qabs_ref[:, h * KV_LORA:(h + 1) * KV_LORA] for h in range(MLA_HEADS)],
        axis=0).astype(BF16)
    lhs = jnp.concatenate([wkt_ref[...], qa], axis=0)
    n_exp = MLA_HEADS * QK_NOPE

    def scores(c, s_rope, mask):
        n = c.shape[0]
        cb = c.astype(BF16)
        big = _dot_nt(lhs, cb)
        kt = big[:n_exp].reshape(MLA_HEADS, QK_NOPE, n)
        inv = lax.rsqrt(jnp.mean(kt * kt, axis=1, keepdims=True) + EPS)
        s = big[n_exp:].reshape(MLA_HEADS, t, n) * inv + s_rope.reshape(MLA_HEADS, t, n)
        if mask is not None:
            s = jnp.where(mask, s, -jnp.inf)
        return s, cb

    def update(carry, s, cb):
        m, l, acc = carry
        n = s.shape[-1]
        m_new = jnp.maximum(m, jnp.max(s, axis=-1, keepdims=True))
        alpha = jnp.exp2(m - m_new)
        p = jnp.exp2(s - m_new)
        l = alpha * l + jnp.sum(p, axis=-1, keepdims=True)
        acc = alpha.reshape(hq, 1) * acc + _dot(p.reshape(hq, n).astype(BF16), cb)
        return m_new, l, acc

    carry = (jnp.full((MLA_HEADS, t, 1), -jnp.inf, F32), jnp.zeros((MLA_HEADS, t, 1), F32),
             jnp.zeros((hq, KV_LORA), F32))
    n_keys = chunk_pages * page
    pending = None
    for j in range(n_pages // chunk_pages):
        c = cbuf[slot, pl.ds(j * chunk_pages, chunk_pages)].reshape(n_keys, KV_LORA)
        krt = jnp.concatenate(
            [krbuf[slot, j * chunk_pages + i] for i in range(chunk_pages)], axis=1)
        nxt = scores(c, _dot(qr, krt.astype(BF16)), None)
        start_fetch(b_ahead, slot_ahead, j * chunk_pages, (j + 1) * chunk_pages)
        if pending is not None:
            carry = update(carry, *pending)
        pending = nxt

    pad = LANES - t
    c = jnp.concatenate([cnew_ref[...], jnp.zeros((pad, KV_LORA), F32)], axis=0)
    kr = jnp.concatenate([krnew_ref[...], jnp.zeros((pad, QK_ROPE), F32)], axis=0)
    shape = (MLA_HEADS, t, LANES)
    mask = lax.broadcasted_iota(jnp.int32, shape, 2) <= lax.broadcasted_iota(jnp.int32, shape, 1)
    last = scores(c, _dot_nt(qr, kr.astype(BF16)), mask)
    carry = update(carry, *pending)
    m, l, acc = update(carry, *last)

    @pl.when(b == nb - 1)
    def _():
        wait_fetch(lax.rem(b + 1, slots))
        wait_fetch(slot_ahead)

    oc = (acc / l.reshape(hq, 1)).astype(BF16)
    full = _dot(oc, wv_ref[...])
    for h in range(MLA_HEADS):
        o_ref[:, h * V_HEAD:(h + 1) * V_HEAD] = full[h * t:(h + 1) * t, h * V_HEAD:(h + 1) * V_HEAD]


def _mla_sample(layer, page_table, q, qabs, cnew, krnew, wkt, wv, cache_ckv, cache_krope_t,
                chunk_pages):
    dec_batch, n_pages = page_table.shape
    t = q.shape[0] // dec_batch
    page = cache_ckv.shape[2]
    row = lambda width: pl.BlockSpec((t, width), lambda b, pt: (b, 0))
    const = functools.partial(_layer_spec, layer)
    grid_spec = pltpu.PrefetchScalarGridSpec(
        num_scalar_prefetch=1, grid=(dec_batch,),
        in_specs=[row(MLA_HEADS * LANES), row(MLA_HEADS * KV_LORA), row(KV_LORA), row(QK_ROPE),
                  const((MLA_HEADS * QK_NOPE, KV_LORA)), const((KV_LORA, MLA_HEADS * V_HEAD)),
                  pl.BlockSpec(memory_space=pl.ANY), pl.BlockSpec(memory_space=pl.ANY)],
        out_specs=row(MLA_HEADS * V_HEAD),
        scratch_shapes=[pltpu.VMEM((SAMPLE_SLOTS, n_pages, page, KV_LORA), F32),
                        pltpu.VMEM((SAMPLE_SLOTS, n_pages, QK_ROPE, page), F32),
                        pltpu.SemaphoreType.DMA((2, SAMPLE_SLOTS))])
    return pl.pallas_call(
        functools.partial(_sattn_kernel, layer, n_pages, chunk_pages, t),
        grid_spec=grid_spec,
        out_shape=jax.ShapeDtypeStruct((dec_batch * t, MLA_HEADS * V_HEAD), F32),
        compiler_params=pltpu.CompilerParams(dimension_semantics=("arbitrary",),
                                             vmem_limit_bytes=VMEM_LIMIT),
        name="mla_sample",
    )(page_table, q, qabs, cnew, krnew, wkt, wv, cache_ckv, cache_krope_t)


def _out_kernel(x_ref, om_ref, y_ref, srg_ref, sga_ref, sgb_ref, gret_ref, wo_ref, ln2_ref,
                wup_ref, wdn_ref, o_ref):
    o_ret = y_ref[...] * gret_ref[...] * srg_ref[...].astype(F32)
    mixed = (sga_ref[...].astype(F32) * om_ref[...].astype(F32)
             + sgb_ref[...].astype(F32) * o_ret)
    x1 = x_ref[...] + _dot(mixed.astype(BF16), wo_ref[...])
    h2 = _rmsnorm(x1, ln2_ref[...]).astype(BF16)
    u = jnp.maximum(_dot(h2, wup_ref[...]), 0.0)
    o_ref[...] = x1 + _dot((u * u).astype(BF16), wdn_ref[...])


def _out_proj(layer, x, om, y, rg, ga, gb, w, tm):
    T = x.shape[0]
    lspec = functools.partial(_layer_spec, layer)
    row = pl.BlockSpec((tm, D_MODEL), lambda i: (i, 0))
    return pl.pallas_call(
        _out_kernel,
        grid=(T // tm,),
        in_specs=[row, row, row, row, row, row, lspec((1, D_MODEL)),
                  lspec((D_MODEL, D_MODEL)), lspec((1, D_MODEL)),
                  lspec((D_MODEL, D_FF)), lspec((D_FF, D_MODEL))],
        out_specs=row,
        out_shape=jax.ShapeDtypeStruct((T, D_MODEL), F32),
        compiler_params=pltpu.CompilerParams(dimension_semantics=("parallel",),
                                             vmem_limit_bytes=VMEM_LIMIT),
        name="out_proj",
    )(x, om, y, rg, ga, gb, w["g_ret"], w["w_o"], w["ln2"], w["w_up"], w["w_down"])


def _prep_weights(ln1, w_in, g_qlat, w_uq, g_qn, g_qr, g_kvlat, g_kr, w_ukv, g_kn, g_ret,
                  w_o, ln2, w_up, w_down):
    depth = w_in.shape[0]
    lat = Q_LORA + KV_LORA
    qk = QK_NOPE + QK_ROPE
    zeros = lambda *shape: jnp.zeros((depth,) + shape, F32)
    w_in_p = jnp.concatenate(
        [w_in[..., :lat], w_in[..., lat + QK_ROPE:], zeros(D_MODEL, ROPE_LO),
         w_in[..., lat:lat + QK_ROPE], zeros(D_MODEL, LANES - ROPE_LO - QK_ROPE)], axis=-1)
    w_uq_p = jnp.pad(w_uq.reshape(depth, Q_LORA, MLA_HEADS, qk),
                     ((0, 0), (0, 0), (0, 0), (0, LANES - qk)))
    kv = w_ukv.reshape(depth, KV_LORA, MLA_HEADS, QK_NOPE + V_HEAD)
    wk, wv = kv[..., :QK_NOPE], kv[..., QK_NOPE:]
    w_uk_p = jnp.pad(wk, ((0, 0), (0, 0), (0, 0), (0, LANES - QK_NOPE)))
    wkt = wk.transpose(0, 2, 3, 1)
    scale = float(qk ** -0.5 * np.log2(np.e))
    vec = lambda v: v.reshape(depth, 1, -1)
    return {
        "ln1": vec(ln1), "w_in": w_in_p.astype(BF16), "g_qlat": vec(g_qlat),
        "w_uq": w_uq_p.reshape(depth, Q_LORA, MLA_HEADS * LANES).astype(BF16),
        "g_q": vec(jnp.concatenate([g_qn, g_qr, zeros(LANES - qk)], axis=-1) * scale),
        "g_kvlat": vec(g_kvlat),
        "g_kr": vec(jnp.concatenate([zeros(ROPE_LO), g_kr, zeros(LANES - ROPE_LO - QK_ROPE)],
                                    axis=-1)),
        "g_k": vec(jnp.concatenate([g_kn, zeros(LANES - QK_NOPE)], axis=-1)),
        "w_uk": w_uk_p.reshape(depth, KV_LORA, MLA_HEADS * LANES).astype(BF16),
        "w_uvt": wv.transpose(0, 2, 3, 1).reshape(depth, MLA_HEADS * V_HEAD, KV_LORA).astype(BF16),
        "w_ukt": jnp.pad(wkt, ((0, 0), (0, 0), (0, LANES - QK_NOPE), (0, 0))).reshape(
            depth, MLA_HEADS * LANES, KV_LORA).astype(BF16),
        "w_ukt_all": wkt.reshape(depth, MLA_HEADS * QK_NOPE, KV_LORA).astype(BF16),
        "w_uv": wv.reshape(depth, KV_LORA, MLA_HEADS * V_HEAD).astype(BF16),
        "gmat": jnp.asarray(_group_mean_matrix(), BF16),
        "g_ret": vec(g_ret), "w_o": w_o.astype(BF16), "ln2": vec(ln2),
        "w_up": w_up.astype(BF16), "w_down": w_down.astype(BF16),
    }


def _rope_tables(pos):
    def cos_sin(dim):
        inv = ROPE_THETA ** (-jnp.arange(0, dim, 2, dtype=F32) / dim)
        ang = pos.astype(F32)[:, None] * inv[None, :]
        return jnp.cos(ang), jnp.sin(ang)
    n = pos.shape[0]
    c_r, s_r = cos_sin(RET_QK)
    c_m, s_m = cos_sin(QK_ROPE)
    tail = jnp.zeros((n, LANES - ROPE_LO - QK_ROPE), F32)
    return {
        "cr": jnp.concatenate([c_r, c_r], axis=1),
        "sr": jnp.concatenate([-s_r, s_r], axis=1),
        "cm": jnp.concatenate([jnp.ones((n, ROPE_LO), F32), c_m, c_m, tail], axis=1),
        "sm": jnp.concatenate([jnp.zeros((n, ROPE_LO), F32), -s_m, s_m, tail], axis=1),
    }


def kernel(x_prompt, x_sample, cache_ckv, cache_krope, state_ret, page_table, ln1, w_in, g_qlat,
           w_uq, g_qn, g_qr, g_kvlat, g_kr, w_ukv, g_kn, g_ret, w_o, ln2, w_up, w_down):
    batch, seq, _ = x_prompt.shape
    dec_batch, dec_seq, _ = x_sample.shape
    depth = w_in.shape[0]
    n_pages = page_table.shape[1]
    past = n_pages * cache_ckv.shape[2]

    tm_p = min(256, seq)
    tm_s = min(256, dec_batch * dec_seq)
    tq = min(512, seq)
    ret_chunk = min(128, seq)
    ret_ts = min(1024, seq)
    chunk_pages = min(16, n_pages)
    bb = min(8, dec_batch)
    cache_krope_t = jnp.swapaxes(cache_krope, 2, 3)
    st_sample = jnp.zeros(state_ret.shape, F32)

    tabs_p = _rope_tables(jnp.arange(seq))
    tabs_s = {k: jnp.tile(v, (tm_s // dec_seq, 1))
              for k, v in _rope_tables(past + jnp.arange(dec_seq)).items()}
    lgtab = jnp.broadcast_to(
        jnp.log1p(-jnp.exp2(-5.0 - jnp.arange(RET_HEADS, dtype=F32)))[:, None, None],
        (RET_HEADS, 8, LANES))

    xp = x_prompt.reshape(batch * seq, D_MODEL)
    xs = x_sample.reshape(dec_batch * dec_seq, D_MODEL)
    outs = [[] for _ in range(5)]
    w = _prep_weights(ln1, w_in, g_qlat, w_uq, g_qn, g_qr, g_kvlat, g_kr, w_ukv, g_kn, g_ret,
                      w_o, ln2, w_up, w_down)
    for l in range(depth):
        q, ckv, kr, rq, rk, rv, rg, ga, gb, k, vt = _in_proj(
            "prompt", l, xp, w, tabs_p, seq // tm_p, tm_p, batch, seq)
        o_mla = _mla_prompt(q, k, vt, batch, seq, tq, 4)
        y, st = _ret_prompt(lgtab, rq, rk, rv, batch, seq, ret_chunk, ret_ts)
        xp = _out_proj(l, xp, o_mla, y, rg, ga, gb, w, tm_p)
        outs[0].append(ckv.reshape(batch, seq, KV_LORA))
        outs[1].append(kr.reshape(batch, seq, QK_ROPE))
        outs[2].append(st)

        q, ckv, kr, rq, rk, rv, rg, ga, gb, qabs = _in_proj(
            "sample", l, xs, w, tabs_s, 1, tm_s, dec_batch, dec_seq)
        o_mla = _mla_sample(l, page_table, q, qabs, ckv, kr, w["w_ukt_all"], w["w_uv"],
                            cache_ckv, cache_krope_t, chunk_pages)
        y, st_sample = _ret_sample(l, lgtab, rq, rk, rv, state_ret, st_sample, bb)
        xs = _out_proj(l, xs, o_mla, y, rg, ga, gb, w, tm_s)
        outs[3].append(ckv.reshape(dec_batch, dec_seq, KV_LORA))
        outs[4].append(kr.reshape(dec_batch, dec_seq, QK_ROPE))

    return (xp.reshape(batch, seq, D_MODEL), xs.reshape(dec_batch, dec_seq, D_MODEL),
            jnp.stack(outs[0]), jnp.stack(outs[1]), jnp.stack(outs[2]),
            jnp.stack(outs[3]), jnp.stack(outs[4]), st_sample)
```

```python
import functools

import numpy as np
import jax
import jax.numpy as jnp
from jax import lax
from jax.experimental import pallas as pl
from jax.experimental.pallas import tpu as pltpu

F32 = jnp.float32
BF16 = jnp.bfloat16

D_MODEL = 1024
MLA_HEADS = 8
V_HEAD = 128
Q_LORA = 384
KV_LORA = 256
QK_NOPE = 64
QK_ROPE = 32
RET_HEADS = 4
RET_V = 256
RET_QK = 128
D_FF = 4096
ROPE_THETA = 10000.0
EPS = 1e-6

LANES = 128
VMEM_LIMIT = 52 << 20
SAMPLE_SLOTS = 4

_SEG = {}
_off = 0
for _name, _w in (("qlat", Q_LORA), ("kvlat", KV_LORA), ("rq", RET_HEADS * RET_QK),
                  ("rk", RET_HEADS * RET_QK), ("rv", RET_HEADS * RET_V), ("rg", D_MODEL),
                  ("ga", D_MODEL), ("gb", D_MODEL), ("kr", LANES)):
    _SEG[_name] = (_off, _off + _w)
    _off += _w
D_IN_P = _off
ROPE_LO = QK_NOPE
ROPE_HALF = QK_ROPE // 2


def _group_mean_matrix():
    gid = np.array([0] * QK_NOPE + [1] * QK_ROPE + [2] * (LANES - QK_NOPE - QK_ROPE))
    size = np.array([QK_NOPE, QK_ROPE, LANES - QK_NOPE - QK_ROPE], np.float32)
    g = (gid[:, None] == gid[None, :]).astype(np.float32) / size[gid][None, :]
    out = np.zeros((2 * LANES, 2 * LANES), np.float32)
    out[:LANES, :LANES] = g
    out[LANES:, LANES:] = g
    return out


def _const_spec(shape):
    nd = len(shape)
    return pl.BlockSpec(shape, lambda *_: (0,) * nd, pipeline_mode=pl.Buffered(1))


def _layer_spec(layer, shape):
    nd = len(shape)
    return pl.BlockSpec((None,) + tuple(shape), lambda *_: (layer,) + (0,) * nd,
                        pipeline_mode=pl.Buffered(1))


def _rmsnorm(x, g):
    return x * lax.rsqrt(jnp.mean(x * x, axis=-1, keepdims=True) + EPS) * g


def _sigmoid(x):
    return 0.5 * jnp.tanh(0.5 * x) + 0.5


def _dot(a, b):
    return jnp.dot(a, b, preferred_element_type=F32)


def _dot_nt(a, b):
    return lax.dot_general(a, b, (((1,), (1,)), ((), ())), preferred_element_type=F32)


def _head_norm(v, g_ref, gain2):
    heads = []
    for p in range(MLA_HEADS // 2):
        vp = v[:, p * 2 * LANES:(p + 1) * 2 * LANES]
        ms = _dot((vp * vp).astype(BF16), g_ref[...])
        n = vp * lax.rsqrt(ms + EPS) * gain2
        heads.append(n[:, :LANES])
        heads.append(n[:, LANES:])
    return heads


def _rope32(n, cm, sm):
    lane = lax.broadcasted_iota(jnp.int32, n.shape, 1)
    rot = jnp.where(lane < ROPE_LO + ROPE_HALF,
                    pltpu.roll(n, LANES - ROPE_HALF, axis=1),
                    pltpu.roll(n, ROPE_HALF, axis=1))
    return n * cm + rot * sm


def _in_kernel(mode, x_ref, ln1_ref, win_ref, gql_ref, wuq_ref, gq_ref, gkvl_ref, gkr_ref,
               g_ref, cr_ref, sr_ref, cm_ref, sm_ref, wk_ref, gk_ref, *rest):
    if mode == "prompt":
        wuvt_ref = rest[0]
        q_o, ckv_o, kr_o, rq_o, rk_o, rv_o, rg_o, ga_o, gb_o, k_o, vt_o = rest[1:]
    else:
        q_o, ckv_o, kr_o, rq_o, rk_o, rv_o, rg_o, ga_o, gb_o, qabs_o = rest

    h = _rmsnorm(x_ref[...], ln1_ref[...]).astype(BF16)

    def seg(name):
        a, b = _SEG[name]
        return _dot(h, win_ref[:, a:b])

    cm = cm_ref[...]
    sm = sm_ref[...]
    cr = cr_ref[...]
    sr = sr_ref[...]

    z_qlat = seg("qlat")
    z_kv = seg("kvlat")
    z_kr = seg("kr")
    rq = seg("rq")
    rk = seg("rk")

    qn = _rmsnorm(z_qlat, gql_ref[...]).astype(BF16)
    q = _dot(qn, wuq_ref[...])
    rv_o[...] = seg("rv").astype(rv_o.dtype)
    gq2 = jnp.concatenate([gq_ref[...], gq_ref[...]], axis=1)
    q_heads = [_rope32(n, cm, sm) for n in _head_norm(q, g_ref, gq2)]
    for hh in range(MLA_HEADS):
        q_o[:, hh * LANES:(hh + 1) * LANES] = q_heads[hh].astype(q_o.dtype)
    z_rg = seg("rg")
    rg_o[...] = (z_rg * _sigmoid(z_rg)).astype(rg_o.dtype)

    c = _rmsnorm(z_kv, gkvl_ref[...])
    ckv_o[...] = c
    cb = c.astype(BF16)
    if mode == "prompt":
        kexp = _dot(cb, wk_ref[...])
        vt = _dot_nt(wuvt_ref[...], cb)
    ga_o[...] = _sigmoid(seg("ga")).astype(ga_o.dtype)
    ms = _dot((z_kr * z_kr).astype(BF16), g_ref[:LANES, :LANES])
    krr = _rope32(z_kr * lax.rsqrt(ms + EPS) * gkr_ref[...], cm, sm)
    kr_o[...] = krr[:, ROPE_LO:ROPE_LO + QK_ROPE]

    if mode == "prompt":
        gk2 = jnp.concatenate([gk_ref[...], gk_ref[...]], axis=1)
        k_heads = _head_norm(kexp, g_ref, gk2)
        for hh in range(MLA_HEADS):
            k_o[:, hh * LANES:(hh + 1) * LANES] = (k_heads[hh] + krr).astype(k_o.dtype)
        vt_o[...] = vt.astype(vt_o.dtype)
    else:
        gk = gk_ref[...]
        for hh in range(MLA_HEADS):
            qg = (q_heads[hh] * gk).astype(BF16)
            qabs_o[:, hh * KV_LORA:(hh + 1) * KV_LORA] = _dot(
                qg, wk_ref[hh * LANES:(hh + 1) * LANES, :]).astype(qabs_o.dtype)
    gb_o[...] = _sigmoid(seg("gb")).astype(gb_o.dtype)

    for hh in range(RET_HEADS):
        sl = slice(hh * RET_QK, (hh + 1) * RET_QK)
        a = rq[:, sl]
        rq_o[:, sl] = (a * cr + pltpu.roll(a, RET_QK // 2, axis=1) * sr).astype(rq_o.dtype)
        b = rk[:, sl]
        rk_o[:, sl] = ((b * cr + pltpu.roll(b, RET_QK // 2, axis=1) * sr)
                       * (RET_QK ** -0.5)).astype(rk_o.dtype)


def _in_proj(mode, layer, x, w, tabs, n_seq_blocks, tm, batch, seq):
    T = x.shape[0]
    lspec = functools.partial(_layer_spec, layer)
    nt = T // tm
    act = BF16 if mode == "prompt" else F32
    row = lambda width: pl.BlockSpec((tm, width), lambda i: (i, 0))
    tab = pl.BlockSpec((tm, LANES), lambda i: (i % n_seq_blocks, 0))
    in_specs = [row(D_MODEL), lspec((1, D_MODEL)), lspec((D_MODEL, D_IN_P)),
                lspec((1, Q_LORA)), lspec((Q_LORA, MLA_HEADS * LANES)),
                lspec((1, LANES)), lspec((1, KV_LORA)), lspec((1, LANES)),
                _const_spec((2 * LANES, 2 * LANES)), tab, tab, tab, tab]
    args = [x, w["ln1"], w["w_in"], w["g_qlat"], w["w_uq"], w["g_q"], w["g_kvlat"], w["g_kr"],
            w["gmat"], tabs["cr"], tabs["sr"], tabs["cm"], tabs["sm"]]
    out_shape = [jax.ShapeDtypeStruct((T, MLA_HEADS * LANES), act),
                 jax.ShapeDtypeStruct((T, KV_LORA), F32),
                 jax.ShapeDtypeStruct((T, QK_ROPE), F32),
                 jax.ShapeDtypeStruct((T, RET_HEADS * RET_QK), act),
                 jax.ShapeDtypeStruct((T, RET_HEADS * RET_QK), act),
                 jax.ShapeDtypeStruct((T, RET_HEADS * RET_V), act),
                 jax.ShapeDtypeStruct((T, D_MODEL), act),
                 jax.ShapeDtypeStruct((T, D_MODEL), act),
                 jax.ShapeDtypeStruct((T, D_MODEL), act)]
    out_specs = [row(MLA_HEADS * LANES), row(KV_LORA), row(QK_ROPE), row(RET_HEADS * RET_QK),
                 row(RET_HEADS * RET_QK), row(RET_HEADS * RET_V), row(D_MODEL), row(D_MODEL),
                 row(D_MODEL)]
    if mode == "prompt":
        in_specs += [lspec((KV_LORA, MLA_HEADS * LANES)), lspec((1, LANES)),
                     lspec((MLA_HEADS * V_HEAD, KV_LORA))]
        args += [w["w_uk"], w["g_k"], w["w_uvt"]]
        out_shape += [jax.ShapeDtypeStruct((T, MLA_HEADS * LANES), BF16),
                      jax.ShapeDtypeStruct((batch, MLA_HEADS * V_HEAD, seq), BF16)]
        out_specs += [row(MLA_HEADS * LANES),
                      pl.BlockSpec((None, MLA_HEADS * V_HEAD, tm),
                                   lambda i: (i // n_seq_blocks, 0, i % n_seq_blocks))]
    else:
        in_specs += [lspec((MLA_HEADS * LANES, KV_LORA)), lspec((1, LANES))]
        args += [w["w_ukt"], w["g_k"]]
        out_shape += [jax.ShapeDtypeStruct((T, MLA_HEADS * KV_LORA), F32)]
        out_specs += [row(MLA_HEADS * KV_LORA)]
    return pl.pallas_call(
        functools.partial(_in_kernel, mode),
        grid=(nt,), in_specs=in_specs, out_specs=out_specs, out_shape=out_shape,
        compiler_params=pltpu.CompilerParams(dimension_semantics=("parallel",),
                                             vmem_limit_bytes=VMEM_LIMIT),
        name="in_proj_" + mode,
    )(*args)


def _flash_kernel(t, nh, q_ref, k_ref, vt_ref, o_ref, s_ref, bm_ref, m_ref, l_ref, acc_ref):
    qi = pl.program_id(2)
    m_ref[...] = jnp.full(m_ref.shape, -jnp.inf, F32)
    l_ref[...] = jnp.zeros(l_ref.shape, F32)
    acc_ref[...] = jnp.zeros(acc_ref.shape, F32)

    def step(cur, nxt, masked):
        if masked:
            kpos = nxt * t + lax.broadcasted_iota(jnp.int32, (t, t), 0)
            qpos = qi * t + lax.broadcasted_iota(jnp.int32, (t, t), 1)
            keep = kpos <= qpos
        for hh in range(nh):
            hs = slice(hh * LANES, (hh + 1) * LANES)
            if cur is not None:
                s_cur = s_ref[hh]
                m_prev = m_ref[hh]
                m_new = jnp.maximum(m_prev, bm_ref[hh])
                alpha = jnp.exp2(m_prev - m_new)
            if nxt is not None:
                n0 = pl.multiple_of(nxt * t, t)
                s = _dot_nt(k_ref[pl.ds(n0, t), hs], q_ref[:, hs])
                if masked:
                    s = jnp.where(keep, s, -jnp.inf)
                s_ref[hh] = s
                bm_ref[hh] = jnp.max(s, axis=0, keepdims=True)
            if cur is not None:
                c0 = pl.multiple_of(cur * t, t)
                p = jnp.exp2(s_cur - m_new)
                l_ref[hh] = alpha * l_ref[hh] + jnp.sum(p, axis=0, keepdims=True)
                acc_ref[hh] = alpha * acc_ref[hh] + _dot(vt_ref[hs, pl.ds(c0, t)], p.astype(BF16))
                m_ref[hh] = m_new

    @pl.when(qi == 0)
    def _():
        step(None, 0, True)

    @pl.when(qi > 0)
    def _():
        step(None, 0, False)

        def body(i, carry):
            step(i, i + 1, False)
            return carry

        lax.fori_loop(0, qi - 1, body, 0)
        step(qi - 1, qi, True)

    step(qi, None, False)
    for hh in range(nh):
        o_ref[:, hh * V_HEAD:(hh + 1) * V_HEAD] = (acc_ref[hh] / l_ref[hh]).T.astype(o_ref.dtype)


def _mla_prompt(q, k, vt, batch, seq, t, nh):
    nq = seq // t
    stat = pltpu.VMEM((nh, 1, t), F32)
    return pl.pallas_call(
        functools.partial(_flash_kernel, t, nh),
        grid=(batch, MLA_HEADS // nh, nq),
        in_specs=[pl.BlockSpec((t, nh * LANES), lambda b, h, i: (b * nq + i, h)),
                  pl.BlockSpec((seq, nh * LANES), lambda b, h, i: (b, h)),
                  pl.BlockSpec((None, nh * V_HEAD, seq), lambda b, h, i: (b, h, 0))],
        out_specs=pl.BlockSpec((t, nh * V_HEAD), lambda b, h, i: (b * nq + i, h)),
        out_shape=jax.ShapeDtypeStruct((batch * seq, MLA_HEADS * V_HEAD), BF16),
        scratch_shapes=[pltpu.VMEM((nh, t, t), F32), stat, stat, stat,
                        pltpu.VMEM((nh, V_HEAD, t), F32)],
        compiler_params=pltpu.CompilerParams(
            dimension_semantics=("parallel", "parallel", "arbitrary"),
            vmem_limit_bytes=VMEM_LIMIT),
        name="mla_prompt",
    )(q, k, vt)


def _group_norm(o):
    mu = jnp.mean(o, axis=-1, keepdims=True)
    d = o - mu
    return d * lax.rsqrt(jnp.mean(d * d, axis=-1, keepdims=True) + EPS)


def _decay_tables(lg, c):
    ti = lax.broadcasted_iota(jnp.int32, (c, c), 0)
    tj = lax.broadcasted_iota(jnp.int32, (c, c), 1)
    diff = (ti - tj).astype(F32)
    decay = jnp.where(diff >= 0, jnp.exp(lg * jnp.maximum(diff, 0.0)), 0.0)
    t = lax.broadcasted_iota(jnp.int32, (c, 1), 0).astype(F32)
    cross = jnp.exp(lg * (t + 1.0))
    end = jnp.exp(lg * (c - 1.0 - t))
    total = jnp.exp(lg * float(c))
    return decay, cross, end, total


def _ret_prompt_kernel(chunk, n_chunks, lg_ref, rq_ref, rk_ref, rv_ref, y_ref, st_ref, s_ref):
    si = pl.program_id(0)
    tables = [_decay_tables(lg_ref[h][0:1, 0:1], chunk) for h in range(RET_HEADS)]
    pairs = [(b, h) for b in range(rq_ref.shape[0]) for h in range(RET_HEADS)]

    @pl.when(si == 0)
    def _():
        s_ref[...] = jnp.zeros(s_ref.shape, F32)

    def body(ci, carry):
        r0 = pl.multiple_of(ci * chunk, chunk)
        first = []
        for b, h in pairs:
            end = tables[h][2]
            q = rq_ref[b, pl.ds(r0, chunk), h * RET_QK:(h + 1) * RET_QK]
            k = rk_ref[b, pl.ds(r0, chunk), h * RET_QK:(h + 1) * RET_QK]
            v = rv_ref[b, pl.ds(r0, chunk), h * RET_V:(h + 1) * RET_V]
            state = s_ref[b, h]
            kd = (k.astype(F32) * end).astype(BF16)
            first.append((_dot_nt(q, k), _dot(q, state.astype(BF16)),
                          lax.dot_general(kd, v, (((0,), (0,)), ((), ())),
                                          preferred_element_type=F32), v, state))
        for (b, h), (qk, qs, kv, v, state) in zip(pairs, first):
            decay, cross, _, total = tables[h]
            o = _dot((qk * decay).astype(BF16), v) + qs * cross
            s_ref[b, h] = total * state + kv
            y_ref[b, pl.ds(r0, chunk), h * RET_V:(h + 1) * RET_V] = _group_norm(o)
        return carry

    lax.fori_loop(0, n_chunks, body, 0)

    @pl.when(si == pl.num_programs(0) - 1)
    def _():
        st_ref[...] = s_ref[...]


def _ret_prompt(lgtab, rq, rk, rv, batch, seq, chunk, ts):
    view = lambda a: a.reshape(batch, seq, a.shape[-1])
    row = lambda width: pl.BlockSpec((batch, ts, width), lambda s: (0, s, 0))
    state = (batch, RET_HEADS, RET_QK, RET_V)
    y, st = pl.pallas_call(
        functools.partial(_ret_prompt_kernel, chunk, ts // chunk),
        grid=(seq // ts,),
        in_specs=[_const_spec((RET_HEADS, 8, LANES)), row(RET_HEADS * RET_QK),
                  row(RET_HEADS * RET_QK), row(RET_HEADS * RET_V)],
        out_specs=[row(RET_HEADS * RET_V), pl.BlockSpec(state, lambda s: (0, 0, 0, 0))],
        out_shape=[jax.ShapeDtypeStruct((batch, seq, RET_HEADS * RET_V), F32),
                   jax.ShapeDtypeStruct(state, F32)],
        scratch_shapes=[pltpu.VMEM(state, F32)],
        compiler_params=pltpu.CompilerParams(dimension_semantics=("arbitrary",),
                                             vmem_limit_bytes=VMEM_LIMIT),
        name="ret_prompt",
    )(lgtab, view(rq), view(rk), view(rv))
    return y.reshape(batch * seq, RET_HEADS * RET_V), st


def _ret_sample_kernel(bb, t, lg_ref, rq_ref, rk_ref, rv_ref, st_ref, stack_hbm, y_ref, stn_ref):
    del stack_hbm
    tables = [_decay_tables(lg_ref[h][0:1, 0:1], t) for h in range(RET_HEADS)]
    pairs = [(h, bi) for h in range(RET_HEADS) for bi in range(bb)]
    first = []
    for h, bi in pairs:
        end = tables[h][2]
        rows = slice(bi * t, (bi + 1) * t)
        q = rq_ref[rows, h * RET_QK:(h + 1) * RET_QK].astype(BF16)
        k = rk_ref[rows, h * RET_QK:(h + 1) * RET_QK]
        v = rv_ref[rows, h * RET_V:(h + 1) * RET_V].astype(BF16).astype(F32)
        state = st_ref[bi, h]
        kd = (k * end).astype(BF16).astype(F32)
        first.append((_dot_nt(q, k.astype(BF16)), _dot(q, state.astype(BF16)), _dot(kd.T, v),
                      v, state))
    for (h, bi), (qk, qs, kv, v, state) in zip(pairs, first):
        decay, cross, _, total = tables[h]
        rows = slice(bi * t, (bi + 1) * t)
        a = (qk * decay).astype(BF16).astype(F32)
        o = qs * cross
        for j in range(t):
            o = o + a[:, j:j + 1] * v[j:j + 1, :]
        y_ref[rows, h * RET_V:(h + 1) * RET_V] = _group_norm(o)
        stn_ref[bi, h] = total * state + kv


def _ret_sample(layer, lgtab, rq, rk, rv, state, stacked, bb):
    dec_batch = state.shape[1]
    t = rq.shape[0] // dec_batch
    row = lambda width: pl.BlockSpec((bb * t, width), lambda i: (i, 0))
    st_spec = pl.BlockSpec((None, bb, RET_HEADS, RET_QK, RET_V), lambda i: (layer, i, 0, 0, 0))
    return pl.pallas_call(
        functools.partial(_ret_sample_kernel, bb, t),
        grid=(dec_batch // bb,),
        in_specs=[_const_spec((RET_HEADS, 8, LANES)), row(RET_HEADS * RET_QK),
                  row(RET_HEADS * RET_QK), row(RET_HEADS * RET_V), st_spec,
                  pl.BlockSpec(memory_space=pl.ANY)],
        out_specs=[row(RET_HEADS * RET_V), st_spec],
        out_shape=[jax.ShapeDtypeStruct((dec_batch * t, RET_HEADS * RET_V), F32),
                   jax.ShapeDtypeStruct(state.shape, F32)],
        input_output_aliases={5: 1},
        compiler_params=pltpu.CompilerParams(dimension_semantics=("parallel",),
                                             vmem_limit_bytes=VMEM_LIMIT),
        name="ret_sample",
    )(lgtab, rq, rk, rv, state, stacked)


def _sattn_kernel(layer, n_pages, chunk_pages, t, pt_ref, q_ref, qabs_ref, cnew_ref, krnew_ref,
                  wkt_ref, wv_ref, cc_hbm, ckrt_hbm, o_ref, cbuf, krbuf, sem):
    b = pl.program_id(0)
    nb = pl.num_programs(0)
    slots = cbuf.shape[0]
    slot = lax.rem(b, slots)
    page = cbuf.shape[2]
    hq = MLA_HEADS * t

    def start_fetch(bb, sl, lo, hi):
        for p in range(lo, hi):
            pg = pt_ref[bb, p]
            pltpu.make_async_copy(cc_hbm.at[layer, pg], cbuf.at[sl, p],
                                  sem.at[0, sl]).start(priority=p % 2)
            pltpu.make_async_copy(ckrt_hbm.at[layer, pg], krbuf.at[sl, p],
                                  sem.at[1, sl]).start(priority=(p + 1) % 2)

    def wait_fetch(sl):
        pages = pl.ds(0, n_pages)
        pltpu.make_async_copy(cc_hbm.at[layer, pages], cbuf.at[sl], sem.at[0, sl]).wait()
        pltpu.make_async_copy(ckrt_hbm.at[layer, pages], krbuf.at[sl], sem.at[1, sl]).wait()

    @pl.when(b == 0)
    def _():
        start_fetch(0, 0, 0, n_pages)
        start_fetch(lax.rem(1, nb), 1, 0, n_pages)

    wait_fetch(slot)
    b_ahead = lax.rem(b + 2, nb)
    slot_ahead = lax.rem(b + 2, slots)

    qh = q_ref[...]
    qr = jnp.concatenate(
        [qh[:, h * LANES + ROPE_LO:h * LANES + ROPE_LO + QK_ROPE] for h in range(MLA_HEADS)],
        axis=0).astype(BF16)
    qa = jnp.concatenate(
        [qabs_ref[:, h * KV_LORA:(h + 1) * KV_LORA] for h in range(MLA_HEADS)],
        axis=0).astype(BF16)
    lhs = jnp.concatenate([wkt_ref[...], qa], axis=0)
    n_exp = MLA_HEADS * QK_NOPE

    def scores(c, s_rope, mask):
        n = c.shape[0]
        cb = c.astype(BF16)
        big = _dot_nt(lhs, cb)
        kt = big[:n_exp].reshape(MLA_HEADS, QK_NOPE, n)
        inv = lax.rsqrt(jnp.mean(kt * kt, axis=1, keepdims=True) + EPS)
        s = big[n_exp:].reshape(MLA_HEADS, t, n) * inv + s_rope.reshape(MLA_HEADS, t, n)
        if mask is not None:
            s = jnp.where(mask, s, -jnp.inf)
        return s, cb

    def update(carry, s, cb):
        m, l, acc = carry
        n = s.shape[-1]
        m_new = jnp.maximum(m, jnp.max(s, axis=-1, keepdims=True))
        alpha = jnp.exp2(m - m_new)
        p = jnp.exp2(s - m_new)
        l = alpha * l + jnp.sum(p, axis=-1, keepdims=True)
        acc = alpha.reshape(hq, 1) * acc + _dot(p.reshape(hq, n).astype(BF16), cb)
        return m_new, l, acc

    carry = (jnp.full((MLA_HEADS, t, 1), -jnp.inf, F32), jnp.zeros((MLA_HEADS, t, 1), F32),
             jnp.zeros((hq, KV_LORA), F32))
    n_keys = chunk_pages * page
    pending = None
    for j in range(n_pages // chunk_pages):
        c = cbuf[slot, pl.ds(j * chunk_pages, chunk_pages)].reshape(n_keys, KV_LORA)
        krt = jnp.concatenate(
            [krbuf[slot, j * chunk_pages + i] for i in range(chunk_pages)], axis=1)
        nxt = scores(c, _dot(qr, krt.astype(BF16)), None)
        start_fetch(b_ahead, slot_ahead, j * chunk_pages, (j + 1) * chunk_pages)
        if pending is not None:
            carry = update(carry, *pending)
        pending = nxt

    pad = LANES - t
    c = jnp.concatenate([cnew_ref[...], jnp.zeros((pad, KV_LORA), F32)], axis=0)
    kr = jnp.concatenate([krnew_ref[...], jnp.zeros((pad, QK_ROPE), F32)], axis=0)
    shape = (MLA_HEADS, t, LANES)
    mask = lax.broadcasted_iota(jnp.int32, shape, 2) <= lax.broadcasted_iota(jnp.int32, shape, 1)
    last = scores(c, _dot_nt(qr, kr.astype(BF16)), mask)
    carry = update(carry, *pending)
    m, l, acc = update(carry, *last)

    @pl.when(b == nb - 1)
    def _():
        wait_fetch(lax.rem(b + 1, slots))
        wait_fetch(slot_ahead)

    oc = (acc / l.reshape(hq, 1)).astype(BF16)
    full = _dot(oc, wv_ref[...])
    for h in range(MLA_HEADS):
        o_ref[:, h * V_HEAD:(h + 1) * V_HEAD] = full[h * t:(h + 1) * t, h * V_HEAD:(h + 1) * V_HEAD]


def _mla_sample(layer, page_table, q, qabs, cnew, krnew, wkt, wv, cache_ckv, cache_krope_t,
                chunk_pages):
    dec_batch, n_pages = page_table.shape
    t = q.shape[0] // dec_batch
    page = cache_ckv.shape[2]
    row = lambda width: pl.BlockSpec((t, width), lambda b, pt: (b, 0))
    const = functools.partial(_layer_spec, layer)
    grid_spec = pltpu.PrefetchScalarGridSpec(
        num_scalar_prefetch=1, grid=(dec_batch,),
        in_specs=[row(MLA_HEADS * LANES), row(MLA_HEADS * KV_LORA), row(KV_LORA), row(QK_ROPE),
                  const((MLA_HEADS * QK_NOPE, KV_LORA)), const((KV_LORA, MLA_HEADS * V_HEAD)),
                  pl.BlockSpec(memory_space=pl.ANY), pl.BlockSpec(memory_space=pl.ANY)],
        out_specs=row(MLA_HEADS * V_HEAD),
        scratch_shapes=[pltpu.VMEM((SAMPLE_SLOTS, n_pages, page, KV_LORA), F32),
                        pltpu.VMEM((SAMPLE_SLOTS, n_pages, QK_ROPE, page), F32),
                        pltpu.SemaphoreType.DMA((2, SAMPLE_SLOTS))])
    return pl.pallas_call(
        functools.partial(_sattn_kernel, layer, n_pages, chunk_pages, t),
        grid_spec=grid_spec,
        out_shape=jax.ShapeDtypeStruct((dec_batch * t, MLA_HEADS * V_HEAD), F32),
        compiler_params=pltpu.CompilerParams(dimension_semantics=("arbitrary",),
                                             vmem_limit_bytes=VMEM_LIMIT),
        name="mla_sample",
    )(page_table, q, qabs, cnew, krnew, wkt, wv, cache_ckv, cache_krope_t)


def _out_kernel(x_ref, om_ref, y_ref, srg_ref, sga_ref, sgb_ref, gret_ref, wo_ref, ln2_ref,
                wup_ref, wdn_ref, o_ref):
    o_ret = y_ref[...] * gret_ref[...] * srg_ref[...].astype(F32)
    mixed = (sga_ref[...].astype(F32) * om_ref[...].astype(F32)
             + sgb_ref[...].astype(F32) * o_ret)
    x1 = x_ref[...] + _dot(mixed.astype(BF16), wo_ref[...])
    h2 = _rmsnorm(x1, ln2_ref[...]).astype(BF16)
    u = jnp.maximum(_dot(h2, wup_ref[...]), 0.0)
    o_ref[...] = x1 + _dot((u * u).astype(BF16), wdn_ref[...])


def _out_proj(layer, x, om, y, rg, ga, gb, w, tm):
    T = x.shape[0]
    lspec = functools.partial(_layer_spec, layer)
    row = pl.BlockSpec((tm, D_MODEL), lambda i: (i, 0))
    return pl.pallas_call(
        _out_kernel,
        grid=(T // tm,),
        in_specs=[row, row, row, row, row, row, lspec((1, D_MODEL)),
                  lspec((D_MODEL, D_MODEL)), lspec((1, D_MODEL)),
                  lspec((D_MODEL, D_FF)), lspec((D_FF, D_MODEL))],
        out_specs=row,
        out_shape=jax.ShapeDtypeStruct((T, D_MODEL), F32),
        compiler_params=pltpu.CompilerParams(dimension_semantics=("parallel",),
                                             vmem_limit_bytes=VMEM_LIMIT),
        name="out_proj",
    )(x, om, y, rg, ga, gb, w["g_ret"], w["w_o"], w["ln2"], w["w_up"], w["w_down"])


def _prep_weights(ln1, w_in, g_qlat, w_uq, g_qn, g_qr, g_kvlat, g_kr, w_ukv, g_kn, g_ret,
                  w_o, ln2, w_up, w_down):
    depth = w_in.shape[0]
    lat = Q_LORA + KV_LORA
    qk = QK_NOPE + QK_ROPE
    zeros = lambda *shape: jnp.zeros((depth,) + shape, F32)
    w_in_p = jnp.concatenate(
        [w_in[..., :lat], w_in[..., lat + QK_ROPE:], zeros(D_MODEL, ROPE_LO),
         w_in[..., lat:lat + QK_ROPE], zeros(D_MODEL, LANES - ROPE_LO - QK_ROPE)], axis=-1)
    w_uq_p = jnp.pad(w_uq.reshape(depth, Q_LORA, MLA_HEADS, qk),
                     ((0, 0), (0, 0), (0, 0), (0, LANES - qk)))
    kv = w_ukv.reshape(depth, KV_LORA, MLA_HEADS, QK_NOPE + V_HEAD)
    wk, wv = kv[..., :QK_NOPE], kv[..., QK_NOPE:]
    w_uk_p = jnp.pad(wk, ((0, 0), (0, 0), (0, 0), (0, LANES - QK_NOPE)))
    wkt = wk.transpose(0, 2, 3, 1)
    scale = float(qk ** -0.5 * np.log2(np.e))
    vec = lambda v: v.reshape(depth, 1, -1)
    return {
        "ln1": vec(ln1), "w_in": w_in_p.astype(BF16), "g_qlat": vec(g_qlat),
        "w_uq": w_uq_p.reshape(depth, Q_LORA, MLA_HEADS * LANES).astype(BF16),
        "g_q": vec(jnp.concatenate([g_qn, g_qr, zeros(LANES - qk)], axis=-1) * scale),
        "g_kvlat": vec(g_kvlat),
        "g_kr": vec(jnp.concatenate([zeros(ROPE_LO), g_kr, zeros(LANES - ROPE_LO - QK_ROPE)],
                                    axis=-1)),
        "g_k": vec(jnp.concatenate([g_kn, zeros(LANES - QK_NOPE)], axis=-1)),
        "w_uk": w_uk_p.reshape(depth, KV_LORA, MLA_HEADS * LANES).astype(BF16),
        "w_uvt": wv.transpose(0, 2, 3, 1).reshape(depth, MLA_HEADS * V_HEAD, KV_LORA).astype(BF16),
        "w_ukt": jnp.pad(wkt, ((0, 0), (0, 0), (0, LANES - QK_NOPE), (0, 0))).reshape(
            depth, MLA_HEADS * LANES, KV_LORA).astype(BF16),
        "w_ukt_all": wkt.reshape(depth, MLA_HEADS * QK_NOPE, KV_LORA).astype(BF16),
        "w_uv": wv.reshape(depth, KV_LORA, MLA_HEADS * V_HEAD).astype(BF16),
        "gmat": jnp.asarray(_group_mean_matrix(), BF16),
        "g_ret": vec(g_ret), "w_o": w_o.astype(BF16), "ln2": vec(ln2),
        "w_up": w_up.astype(BF16), "w_down": w_down.astype(BF16),
    }


def _rope_tables(pos):
    def cos_sin(dim):
        inv = ROPE_THETA ** (-jnp.arange(0, dim, 2, dtype=F32) / dim)
        ang = pos.astype(F32)[:, None] * inv[None, :]
        return jnp.cos(ang), jnp.sin(ang)
    n = pos.shape[0]
    c_r, s_r = cos_sin(RET_QK)
    c_m, s_m = cos_sin(QK_ROPE)
    tail = jnp.zeros((n, LANES - ROPE_LO - QK_ROPE), F32)
    return {
        "cr": jnp.concatenate([c_r, c_r], axis=1),
        "sr": jnp.concatenate([-s_r, s_r], axis=1),
        "cm": jnp.concatenate([jnp.ones((n, ROPE_LO), F32), c_m, c_m, tail], axis=1),
        "sm": jnp.concatenate([jnp.zeros((n, ROPE_LO), F32), -s_m, s_m, tail], axis=1),
    }


def kernel(x_prompt, x_sample, cache_ckv, cache_krope, state_ret, page_table, ln1, w_in, g_qlat,
           w_uq, g_qn, g_qr, g_kvlat, g_kr, w_ukv, g_kn, g_ret, w_o, ln2, w_up, w_down):
    batch, seq, _ = x_prompt.shape
    dec_batch, dec_seq, _ = x_sample.shape
    depth = w_in.shape[0]
    n_pages = page_table.shape[1]
    past = n_pages * cache_ckv.shape[2]

    tm_p = min(256, seq)
    tm_out = min(512, seq)
    tm_s = min(256, dec_batch * dec_seq)
    tq = min(512, seq)
    ret_chunk = min(128, seq)
    ret_ts = min(1024, seq)
    chunk_pages = min(16, n_pages)
    bb = min(8, dec_batch)
    cache_krope_t = jnp.swapaxes(cache_krope, 2, 3)
    st_sample = jnp.zeros(state_ret.shape, F32)

    tabs_p = _rope_tables(jnp.arange(seq))
    tabs_s = {k: jnp.tile(v, (tm_s // dec_seq, 1))
              for k, v in _rope_tables(past + jnp.arange(dec_seq)).items()}
    lgtab = jnp.broadcast_to(
        jnp.log1p(-jnp.exp2(-5.0 - jnp.arange(RET_HEADS, dtype=F32)))[:, None, None],
        (RET_HEADS, 8, LANES))

    xp = x_prompt.reshape(batch * seq, D_MODEL)
    xs = x_sample.reshape(dec_batch * dec_seq, D_MODEL)
    outs = [[] for _ in range(5)]
    w = _prep_weights(ln1, w_in, g_qlat, w_uq, g_qn, g_qr, g_kvlat, g_kr, w_ukv, g_kn, g_ret,
                      w_o, ln2, w_up, w_down)
    for l in range(depth):
        q, ckv, kr, rq, rk, rv, rg, ga, gb, k, vt = _in_proj(
            "prompt", l, xp, w, tabs_p, seq // tm_p, tm_p, batch, seq)
        o_mla = _mla_prompt(q, k, vt, batch, seq, tq, 4)
        y, st = _ret_prompt(lgtab, rq, rk, rv, batch, seq, ret_chunk, ret_ts)
        xp = _out_proj(l, xp, o_mla, y, rg, ga, gb, w, tm_out)
        outs[0].append(ckv.reshape(batch, seq, KV_LORA))
        outs[1].append(kr.reshape(batch, seq, QK_ROPE))
        outs[2].append(st)

        q, ckv, kr, rq, rk, rv, rg, ga, gb, qabs = _in_proj(
            "sample", l, xs, w, tabs_s, 1, tm_s, dec_batch, dec_seq)
        o_mla = _mla_sample(l, page_table, q, qabs, ckv, kr, w["w_ukt_all"], w["w_uv"],
                            cache_ckv, cache_krope_t, chunk_pages)
        y, st_sample = _ret_sample(l, lgtab, rq, rk, rv, state_ret, st_sample, bb)
        xs = _out_proj(l, xs, o_mla, y, rg, ga, gb, w, tm_s)
        outs[3].append(ckv.reshape(dec_batch, dec_seq, KV_LORA))
        outs[4].append(kr.reshape(dec_batch, dec_seq, QK_ROPE))

    return (xp.reshape(batch, seq, D_MODEL), xs.reshape(dec_batch, dec_seq, D_MODEL),
            jnp.stack(outs[0]), jnp.stack(outs[1]), jnp.stack(outs[2]),
            jnp.stack(outs[3]), jnp.stack(outs[4]), st_sample)
```

```python
import functools

import numpy as np
import jax
import jax.numpy as jnp
from jax import lax
from jax.experimental import pallas as pl
from jax.experimental.pallas import tpu as pltpu

F32 = jnp.float32
BF16 = jnp.bfloat16

D_MODEL = 1024
MLA_HEADS = 8
V_HEAD = 128
Q_LORA = 384
KV_LORA = 256
QK_NOPE = 64
QK_ROPE = 32
RET_HEADS = 4
RET_V = 256
RET_QK = 128
D_FF = 4096
ROPE_THETA = 10000.0
EPS = 1e-6

LANES = 128
VMEM_LIMIT = 52 << 20
SAMPLE_SLOTS = 4

_SEG = {}
_off = 0
for _name, _w in (("qlat", Q_LORA), ("kvlat", KV_LORA), ("rq", RET_HEADS * RET_QK),
                  ("rk", RET_HEADS * RET_QK), ("rv", RET_HEADS * RET_V), ("rg", D_MODEL),
                  ("ga", D_MODEL), ("gb", D_MODEL), ("kr", LANES)):
    _SEG[_name] = (_off, _off + _w)
    _off += _w
D_IN_P = _off
ROPE_LO = QK_NOPE
ROPE_HALF = QK_ROPE // 2


def _group_mean_matrix():
    gid = np.array([0] * QK_NOPE + [1] * QK_ROPE + [2] * (LANES - QK_NOPE - QK_ROPE))
    size = np.array([QK_NOPE, QK_ROPE, LANES - QK_NOPE - QK_ROPE], np.float32)
    g = (gid[:, None] == gid[None, :]).astype(np.float32) / size[gid][None, :]
    out = np.zeros((2 * LANES, 2 * LANES), np.float32)
    out[:LANES, :LANES] = g
    out[LANES:, LANES:] = g
    return out


def _const_spec(shape):
    nd = len(shape)
    return pl.BlockSpec(shape, lambda *_: (0,) * nd, pipeline_mode=pl.Buffered(1))


def _layer_spec(layer, shape):
    nd = len(shape)
    return pl.BlockSpec((None,) + tuple(shape), lambda *_: (layer,) + (0,) * nd,
                        pipeline_mode=pl.Buffered(1))


def _rmsnorm(x, g):
    return x * lax.rsqrt(jnp.mean(x * x, axis=-1, keepdims=True) + EPS) * g


def _sigmoid(x):
    return 0.5 * jnp.tanh(0.5 * x) + 0.5


def _dot(a, b):
    return jnp.dot(a, b, preferred_element_type=F32)


def _dot_nt(a, b):
    return lax.dot_general(a, b, (((1,), (1,)), ((), ())), preferred_element_type=F32)


def _head_norm(v, g_ref, gain2):
    heads = []
    for p in range(MLA_HEADS // 2):
        vp = v[:, p * 2 * LANES:(p + 1) * 2 * LANES]
        ms = _dot((vp * vp).astype(BF16), g_ref[...])
        n = vp * lax.rsqrt(ms + EPS) * gain2
        heads.append(n[:, :LANES])
        heads.append(n[:, LANES:])
    return heads


def _rope32(n, cm, sm):
    lane = lax.broadcasted_iota(jnp.int32, n.shape, 1)
    rot = jnp.where(lane < ROPE_LO + ROPE_HALF,
                    pltpu.roll(n, LANES - ROPE_HALF, axis=1),
                    pltpu.roll(n, ROPE_HALF, axis=1))
    return n * cm + rot * sm


def _in_kernel(mode, x_ref, ln1_ref, win_ref, gql_ref, wuq_ref, gq_ref, gkvl_ref, gkr_ref,
               g_ref, cr_ref, sr_ref, cm_ref, sm_ref, wk_ref, gk_ref, *rest):
    if mode == "prompt":
        wuvt_ref = rest[0]
        q_o, ckv_o, kr_o, rq_o, rk_o, rv_o, rg_o, ga_o, gb_o, k_o, vt_o = rest[1:]
    else:
        q_o, ckv_o, kr_o, rq_o, rk_o, rv_o, rg_o, ga_o, gb_o, qabs_o = rest

    h = _rmsnorm(x_ref[...], ln1_ref[...]).astype(BF16)

    def seg(name):
        a, b = _SEG[name]
        return _dot(h, win_ref[:, a:b])

    cm = cm_ref[...]
    sm = sm_ref[...]
    cr = cr_ref[...]
    sr = sr_ref[...]

    z_qlat = seg("qlat")
    z_kv = seg("kvlat")
    z_kr = seg("kr")
    rq = seg("rq")
    rk = seg("rk")

    qn = _rmsnorm(z_qlat, gql_ref[...]).astype(BF16)
    q = _dot(qn, wuq_ref[...])
    rv_o[...] = seg("rv").astype(rv_o.dtype)
    gq2 = jnp.concatenate([gq_ref[...], gq_ref[...]], axis=1)
    q_heads = [_rope32(n, cm, sm) for n in _head_norm(q, g_ref, gq2)]
    for hh in range(MLA_HEADS):
        q_o[:, hh * LANES:(hh + 1) * LANES] = q_heads[hh].astype(q_o.dtype)
    z_rg = seg("rg")
    rg_o[...] = (z_rg * _sigmoid(z_rg)).astype(rg_o.dtype)

    c = _rmsnorm(z_kv, gkvl_ref[...])
    ckv_o[...] = c
    cb = c.astype(BF16)
    if mode == "prompt":
        kexp = _dot(cb, wk_ref[...])
        vt = _dot_nt(wuvt_ref[...], cb)
    ga_o[...] = _sigmoid(seg("ga")).astype(ga_o.dtype)
    ms = _dot((z_kr * z_kr).astype(BF16), g_ref[:LANES, :LANES])
    krr = _rope32(z_kr * lax.rsqrt(ms + EPS) * gkr_ref[...], cm, sm)
    kr_o[...] = krr[:, ROPE_LO:ROPE_LO + QK_ROPE]

    if mode == "prompt":
        gk2 = jnp.concatenate([gk_ref[...], gk_ref[...]], axis=1)
        k_heads = _head_norm(kexp, g_ref, gk2)
        for hh in range(MLA_HEADS):
            k_o[:, hh * LANES:(hh + 1) * LANES] = (k_heads[hh] + krr).astype(k_o.dtype)
        vt_o[...] = vt.astype(vt_o.dtype)
    else:
        gk = gk_ref[...]
        for hh in range(MLA_HEADS):
            qg = (q_heads[hh] * gk).astype(BF16)
            qabs_o[:, hh * KV_LORA:(hh + 1) * KV_LORA] = _dot(
                qg, wk_ref[hh * LANES:(hh + 1) * LANES, :]).astype(qabs_o.dtype)
    gb_o[...] = _sigmoid(seg("gb")).astype(gb_o.dtype)

    for hh in range(RET_HEADS):
        sl = slice(hh * RET_QK, (hh + 1) * RET_QK)
        a = rq[:, sl]
        rq_o[:, sl] = (a * cr + pltpu.roll(a, RET_QK // 2, axis=1) * sr).astype(rq_o.dtype)
        b = rk[:, sl]
        rk_o[:, sl] = ((b * cr + pltpu.roll(b, RET_QK // 2, axis=1) * sr)
                       * (RET_QK ** -0.5)).astype(rk_o.dtype)


def _in_proj(mode, layer, x, w, tabs, n_seq_blocks, tm, batch, seq):
    T = x.shape[0]
    lspec = functools.partial(_layer_spec, layer)
    nt = T // tm
    act = BF16 if mode == "prompt" else F32
    row = lambda width: pl.BlockSpec((tm, width), lambda i: (i, 0))
    tab = pl.BlockSpec((tm, LANES), lambda i: (i % n_seq_blocks, 0))
    in_specs = [row(D_MODEL), lspec((1, D_MODEL)), lspec((D_MODEL, D_IN_P)),
                lspec((1, Q_LORA)), lspec((Q_LORA, MLA_HEADS * LANES)),
                lspec((1, LANES)), lspec((1, KV_LORA)), lspec((1, LANES)),
                _const_spec((2 * LANES, 2 * LANES)), tab, tab, tab, tab]
    args = [x, w["ln1"], w["w_in"], w["g_qlat"], w["w_uq"], w["g_q"], w["g_kvlat"], w["g_kr"],
            w["gmat"], tabs["cr"], tabs["sr"], tabs["cm"], tabs["sm"]]
    out_shape = [jax.ShapeDtypeStruct((T, MLA_HEADS * LANES), act),
                 jax.ShapeDtypeStruct((T, KV_LORA), F32),
                 jax.ShapeDtypeStruct((T, QK_ROPE), F32),
                 jax.ShapeDtypeStruct((T, RET_HEADS * RET_QK), act),
                 jax.ShapeDtypeStruct((T, RET_HEADS * RET_QK), act),
                 jax.ShapeDtypeStruct((T, RET_HEADS * RET_V), act),
                 jax.ShapeDtypeStruct((T, D_MODEL), act),
                 jax.ShapeDtypeStruct((T, D_MODEL), act),
                 jax.ShapeDtypeStruct((T, D_MODEL), act)]
    out_specs = [row(MLA_HEADS * LANES), row(KV_LORA), row(QK_ROPE), row(RET_HEADS * RET_QK),
                 row(RET_HEADS * RET_QK), row(RET_HEADS * RET_V), row(D_MODEL), row(D_MODEL),
                 row(D_MODEL)]
    if mode == "prompt":
        in_specs += [lspec((KV_LORA, MLA_HEADS * LANES)), lspec((1, LANES)),
                     lspec((MLA_HEADS * V_HEAD, KV_LORA))]
        args += [w["w_uk"], w["g_k"], w["w_uvt"]]
        out_shape += [jax.ShapeDtypeStruct((T, MLA_HEADS * LANES), BF16),
                      jax.ShapeDtypeStruct((batch, MLA_HEADS * V_HEAD, seq), BF16)]
        out_specs += [row(MLA_HEADS * LANES),
                      pl.BlockSpec((None, MLA_HEADS * V_HEAD, tm),
                                   lambda i: (i // n_seq_blocks, 0, i % n_seq_blocks))]
    else:
        in_specs += [lspec((MLA_HEADS * LANES, KV_LORA)), lspec((1, LANES))]
        args += [w["w_ukt"], w["g_k"]]
        out_shape += [jax.ShapeDtypeStruct((T, MLA_HEADS * KV_LORA), F32)]
        out_specs += [row(MLA_HEADS * KV_LORA)]
    return pl.pallas_call(
        functools.partial(_in_kernel, mode),
        grid=(nt,), in_specs=in_specs, out_specs=out_specs, out_shape=out_shape,
        compiler_params=pltpu.CompilerParams(dimension_semantics=("parallel",),
                                             vmem_limit_bytes=VMEM_LIMIT),
        name="in_proj_" + mode,
    )(*args)


def _flash_kernel(t, nh, q_ref, k_ref, vt_ref, o_ref, s_ref, bm_ref, m_ref, l_ref, acc_ref):
    qi = pl.program_id(2)
    m_ref[...] = jnp.full(m_ref.shape, -jnp.inf, F32)
    l_ref[...] = jnp.zeros(l_ref.shape, F32)
    acc_ref[...] = jnp.zeros(acc_ref.shape, F32)

    def step(cur, nxt, masked):
        if masked:
            kpos = nxt * t + lax.broadcasted_iota(jnp.int32, (t, t), 0)
            qpos = qi * t + lax.broadcasted_iota(jnp.int32, (t, t), 1)
            keep = kpos <= qpos
        for hh in range(nh):
            hs = slice(hh * LANES, (hh + 1) * LANES)
            if cur is not None:
                s_cur = s_ref[hh]
                m_prev = m_ref[hh]
                m_new = jnp.maximum(m_prev, bm_ref[hh])
                alpha = jnp.exp2(m_prev - m_new)
            if nxt is not None:
                n0 = pl.multiple_of(nxt * t, t)
                s = _dot_nt(k_ref[pl.ds(n0, t), hs], q_ref[:, hs])
                if masked:
                    s = jnp.where(keep, s, -jnp.inf)
                s_ref[hh] = s
                bm_ref[hh] = jnp.max(s, axis=0, keepdims=True)
            if cur is not None:
                c0 = pl.multiple_of(cur * t, t)
                p = jnp.exp2(s_cur - m_new)
                l_ref[hh] = alpha * l_ref[hh] + jnp.sum(p, axis=0, keepdims=True)
                acc_ref[hh] = alpha * acc_ref[hh] + _dot(vt_ref[hs, pl.ds(c0, t)], p.astype(BF16))
                m_ref[hh] = m_new

    @pl.when(qi == 0)
    def _():
        step(None, 0, True)

    @pl.when(qi > 0)
    def _():
        step(None, 0, False)

        def body(i, carry):
            step(i, i + 1, False)
            return carry

        lax.fori_loop(0, qi - 1, body, 0)
        step(qi - 1, qi, True)

    step(qi, None, False)
    for hh in range(nh):
        o_ref[:, hh * V_HEAD:(hh + 1) * V_HEAD] = (acc_ref[hh] / l_ref[hh]).T.astype(o_ref.dtype)


def _mla_prompt(q, k, vt, batch, seq, t, nh):
    nq = seq // t
    stat = pltpu.VMEM((nh, 1, t), F32)
    return pl.pallas_call(
        functools.partial(_flash_kernel, t, nh),
        grid=(batch, MLA_HEADS // nh, nq),
        in_specs=[pl.BlockSpec((t, nh * LANES), lambda b, h, i: (b * nq + i, h)),
                  pl.BlockSpec((seq, nh * LANES), lambda b, h, i: (b, h)),
                  pl.BlockSpec((None, nh * V_HEAD, seq), lambda b, h, i: (b, h, 0))],
        out_specs=pl.BlockSpec((t, nh * V_HEAD), lambda b, h, i: (b * nq + i, h)),
        out_shape=jax.ShapeDtypeStruct((batch * seq, MLA_HEADS * V_HEAD), BF16),
        scratch_shapes=[pltpu.VMEM((nh, t, t), F32), stat, stat, stat,
                        pltpu.VMEM((nh, V_HEAD, t), F32)],
        compiler_params=pltpu.CompilerParams(
            dimension_semantics=("parallel", "parallel", "arbitrary"),
            vmem_limit_bytes=VMEM_LIMIT),
        name="mla_prompt",
    )(q, k, vt)


def _group_norm(o):
    mu = jnp.mean(o, axis=-1, keepdims=True)
    d = o - mu
    return d * lax.rsqrt(jnp.mean(d * d, axis=-1, keepdims=True) + EPS)


def _decay_tables(lg, c):
    ti = lax.broadcasted_iota(jnp.int32, (c, c), 0)
    tj = lax.broadcasted_iota(jnp.int32, (c, c), 1)
    diff = (ti - tj).astype(F32)
    decay = jnp.where(diff >= 0, jnp.exp(lg * jnp.maximum(diff, 0.0)), 0.0)
    t = lax.broadcasted_iota(jnp.int32, (c, 1), 0).astype(F32)
    cross = jnp.exp(lg * (t + 1.0))
    end = jnp.exp(lg * (c - 1.0 - t))
    total = jnp.exp(lg * float(c))
    return decay, cross, end, total


def _ret_prompt_kernel(chunk, n_chunks, lg_ref, rq_ref, rk_ref, rv_ref, y_ref, st_ref, s_ref):
    si = pl.program_id(0)
    tables = [_decay_tables(lg_ref[h][0:1, 0:1], chunk) for h in range(RET_HEADS)]
    pairs = [(b, h) for b in range(rq_ref.shape[0]) for h in range(RET_HEADS)]

    @pl.when(si == 0)
    def _():
        s_ref[...] = jnp.zeros(s_ref.shape, F32)

    def body(ci, carry):
        r0 = pl.multiple_of(ci * chunk, chunk)
        first = []
        for b, h in pairs:
            end = tables[h][2]
            q = rq_ref[b, pl.ds(r0, chunk), h * RET_QK:(h + 1) * RET_QK]
            k = rk_ref[b, pl.ds(r0, chunk), h * RET_QK:(h + 1) * RET_QK]
            v = rv_ref[b, pl.ds(r0, chunk), h * RET_V:(h + 1) * RET_V]
            state = s_ref[b, h]
            kd = (k.astype(F32) * end).astype(BF16)
            first.append((_dot_nt(q, k), _dot(q, state.astype(BF16)),
                          lax.dot_general(kd, v, (((0,), (0,)), ((), ())),
                                          preferred_element_type=F32), v, state))
        for (b, h), (qk, qs, kv, v, state) in zip(pairs, first):
            decay, cross, _, total = tables[h]
            o = _dot((qk * decay).astype(BF16), v) + qs * cross
            s_ref[b, h] = total * state + kv
            y_ref[b, pl.ds(r0, chunk), h * RET_V:(h + 1) * RET_V] = _group_norm(o)
        return carry

    lax.fori_loop(0, n_chunks, body, 0)

    @pl.when(si == pl.num_programs(0) - 1)
    def _():
        st_ref[...] = s_ref[...]


def _ret_prompt(lgtab, rq, rk, rv, batch, seq, chunk, ts):
    view = lambda a: a.reshape(batch, seq, a.shape[-1])
    row = lambda width: pl.BlockSpec((batch, ts, width), lambda s: (0, s, 0))
    state = (batch, RET_HEADS, RET_QK, RET_V)
    y, st = pl.pallas_call(
        functools.partial(_ret_prompt_kernel, chunk, ts // chunk),
        grid=(seq // ts,),
        in_specs=[_const_spec((RET_HEADS, 8, LANES)), row(RET_HEADS * RET_QK),
                  row(RET_HEADS * RET_QK), row(RET_HEADS * RET_V)],
        out_specs=[row(RET_HEADS * RET_V), pl.BlockSpec(state, lambda s: (0, 0, 0, 0))],
        out_shape=[jax.ShapeDtypeStruct((batch, seq, RET_HEADS * RET_V), F32),
                   jax.ShapeDtypeStruct(state, F32)],
        scratch_shapes=[pltpu.VMEM(state, F32)],
        compiler_params=pltpu.CompilerParams(dimension_semantics=("arbitrary",),
                                             vmem_limit_bytes=VMEM_LIMIT),
        name="ret_prompt",
    )(lgtab, view(rq), view(rk), view(rv))
    return y.reshape(batch * seq, RET_HEADS * RET_V), st


def _ret_sample_kernel(bb, t, lg_ref, rq_ref, rk_ref, rv_ref, st_ref, stack_hbm, y_ref, stn_ref):
    del stack_hbm
    tables = [_decay_tables(lg_ref[h][0:1, 0:1], t) for h in range(RET_HEADS)]
    pairs = [(h, bi) for h in range(RET_HEADS) for bi in range(bb)]
    first = []
    for h, bi in pairs:
        end = tables[h][2]
        rows = slice(bi * t, (bi + 1) * t)
        q = rq_ref[rows, h * RET_QK:(h + 1) * RET_QK].astype(BF16)
        k = rk_ref[rows, h * RET_QK:(h + 1) * RET_QK]
        v = rv_ref[rows, h * RET_V:(h + 1) * RET_V].astype(BF16).astype(F32)
        state = st_ref[bi, h]
        kd = (k * end).astype(BF16).astype(F32)
        first.append((_dot_nt(q, k.astype(BF16)), _dot(q, state.astype(BF16)), _dot(kd.T, v),
                      v, state))
    for (h, bi), (qk, qs, kv, v, state) in zip(pairs, first):
        decay, cross, _, total = tables[h]
        rows = slice(bi * t, (bi + 1) * t)
        a = (qk * decay).astype(BF16).astype(F32)
        o = qs * cross
        for j in range(t):
            o = o + a[:, j:j + 1] * v[j:j + 1, :]
        y_ref[rows, h * RET_V:(h + 1) * RET_V] = _group_norm(o)
        stn_ref[bi, h] = total * state + kv


def _ret_sample(layer, lgtab, rq, rk, rv, state, stacked, bb):
    dec_batch = state.shape[1]
    t = rq.shape[0] // dec_batch
    row = lambda width: pl.BlockSpec((bb * t, width), lambda i: (i, 0))
    st_spec = pl.BlockSpec((None, bb, RET_HEADS, RET_QK, RET_V), lambda i: (layer, i, 0, 0, 0))
    return pl.pallas_call(
        functools.partial(_ret_sample_kernel, bb, t),
        grid=(dec_batch // bb,),
        in_specs=[_const_spec((RET_HEADS, 8, LANES)), row(RET_HEADS * RET_QK),
                  row(RET_HEADS * RET_QK), row(RET_HEADS * RET_V), st_spec,
                  pl.BlockSpec(memory_space=pl.ANY)],
        out_specs=[row(RET_HEADS * RET_V), st_spec],
        out_shape=[jax.ShapeDtypeStruct((dec_batch * t, RET_HEADS * RET_V), F32),
                   jax.ShapeDtypeStruct(state.shape, F32)],
        input_output_aliases={5: 1},
        compiler_params=pltpu.CompilerParams(dimension_semantics=("parallel",),
                                             vmem_limit_bytes=VMEM_LIMIT),
        name="ret_sample",
    )(lgtab, rq, rk, rv, state, stacked)


def _sattn_kernel(layer, n_pages, chunk_pages, t, pt_ref, q_ref, qabs_ref, cnew_ref, krnew_ref,
                  wkt_ref, wv_ref, cc_hbm, ckrt_hbm, o_ref, cbuf, krbuf, sem):
    b = pl.program_id(0)
    nb = pl.num_programs(0)
    slots = cbuf.shape[0]
    slot = lax.rem(b, slots)
    page = cbuf.shape[2]
    hq = MLA_HEADS * t

    def start_fetch(bb, sl, lo, hi):
        for p in range(lo, hi):
            pg = pt_ref[bb, p]
            pltpu.make_async_copy(cc_hbm.at[layer, pg], cbuf.at[sl, p],
                                  sem.at[0, sl]).start(priority=p % 2)
            pltpu.make_async_copy(ckrt_hbm.at[layer, pg], krbuf.at[sl, p],
                                  sem.at[1, sl]).start(priority=(p + 1) % 2)

    def wait_fetch(sl):
        pages = pl.ds(0, n_pages)
        pltpu.make_async_copy(cc_hbm.at[layer, pages], cbuf.at[sl], sem.at[0, sl]).wait()
        pltpu.make_async_copy(ckrt_hbm.at[layer, pages], krbuf.at[sl], sem.at[1, sl]).wait()

    @pl.when(b == 0)
    def _():
        start_fetch(0, 0, 0, n_pages)
        start_fetch(lax.rem(1, nb), 1, 0, n_pages)

    wait_fetch(slot)
    b_ahead = lax.rem(b + 2, nb)
    slot_ahead = lax.rem(b + 2, slots)

    qh = q_ref[...]
    qr = jnp.concatenate(
        [qh[:, h * LANES + ROPE_LO:h * LANES + ROPE_LO + QK_ROPE] for h in range(MLA_HEADS)],
        axis=0).astype(BF16)
    qa = jnp.concatenate(
        [qabs_ref[:, h * KV_LORA:(h + 1) * KV_LORA] for h in range(MLA_HEADS)],
        axis=0).astype(BF16)
    lhs = jnp.concatenate([wkt_ref[...], qa], axis=0)
    n_exp = MLA_HEADS * QK_NOPE

    def scores(c, s_rope, mask):
        n = c.shape[0]
        cb = c.astype(BF16)
        big = _dot_nt(lhs, cb)
        kt = big[:n_exp].reshape(MLA_HEADS, QK_NOPE, n)
        inv = lax.rsqrt(jnp.mean(kt * kt, axis=1, keepdims=True) + EPS)
        s = big[n_exp:].reshape(MLA_HEADS, t, n) * inv + s_rope.reshape(MLA_HEADS, t, n)
        if mask is not None:
            s = jnp.where(mask, s, -jnp.inf)
        return s, cb

    def update(carry, s, cb):
        m, l, acc = carry
        n = s.shape[-1]
        m_new = jnp.maximum(m, jnp.max(s, axis=-1, keepdims=True))
        alpha = jnp.exp2(m - m_new)
        p = jnp.exp2(s - m_new)
        l = alpha * l + jnp.sum(p, axis=-1, keepdims=True)
        acc = alpha.reshape(hq, 1) * acc + _dot(p.reshape(hq, n).astype(BF16), cb)
        return m_new, l, acc

    carry = (jnp.full((MLA_HEADS, t, 1), -jnp.inf, F32), jnp.zeros((MLA_HEADS, t, 1), F32),
             jnp.zeros((hq, KV_LORA), F32))
    n_keys = chunk_pages * page
    pending = None
    for j in range(n_pages // chunk_pages):
        c = cbuf[slot, pl.ds(j * chunk_pages, chunk_pages)].reshape(n_keys, KV_LORA)
        krt = jnp.concatenate(
            [krbuf[slot, j * chunk_pages + i] for i in range(chunk_pages)], axis=1)
        nxt = scores(c, _dot(qr, krt.astype(BF16)), None)
        start_fetch(b_ahead, slot_ahead, j * chunk_pages, (j + 1) * chunk_pages)
        if pending is not None:
            carry = update(carry, *pending)
        pending = nxt

    pad = LANES - t
    c = jnp.concatenate([cnew_ref[...], jnp.zeros((pad, KV_LORA), F32)], axis=0)
    kr = jnp.concatenate([krnew_ref[...], jnp.zeros((pad, QK_ROPE), F32)], axis=0)
    shape = (MLA_HEADS, t, LANES)
    mask = lax.broadcasted_iota(jnp.int32, shape, 2) <= lax.broadcasted_iota(jnp.int32, shape, 1)
    last = scores(c, _dot_nt(qr, kr.astype(BF16)), mask)
    carry = update(carry, *pending)
    m, l, acc = update(carry, *last)

    @pl.when(b == nb - 1)
    def _():
        wait_fetch(lax.rem(b + 1, slots))
        wait_fetch(slot_ahead)

    oc = (acc / l.reshape(hq, 1)).astype(BF16)
    full = _dot(oc, wv_ref[...])
    for h in range(MLA_HEADS):
        o_ref[:, h * V_HEAD:(h + 1) * V_HEAD] = full[h * t:(h + 1) * t, h * V_HEAD:(h + 1) * V_HEAD]


def _mla_sample(layer, page_table, q, qabs, cnew, krnew, wkt, wv, cache_ckv, cache_krope_t,
                chunk_pages):
    dec_batch, n_pages = page_table.shape
    t = q.shape[0] // dec_batch
    page = cache_ckv.shape[2]
    row = lambda width: pl.BlockSpec((t, width), lambda b, pt: (b, 0))
    const = functools.partial(_layer_spec, layer)
    grid_spec = pltpu.PrefetchScalarGridSpec(
        num_scalar_prefetch=1, grid=(dec_batch,),
        in_specs=[row(MLA_HEADS * LANES), row(MLA_HEADS * KV_LORA), row(KV_LORA), row(QK_ROPE),
                  const((MLA_HEADS * QK_NOPE, KV_LORA)), const((KV_LORA, MLA_HEADS * V_HEAD)),
                  pl.BlockSpec(memory_space=pl.ANY), pl.BlockSpec(memory_space=pl.ANY)],
        out_specs=row(MLA_HEADS * V_HEAD),
        scratch_shapes=[pltpu.VMEM((SAMPLE_SLOTS, n_pages, page, KV_LORA), F32),
                        pltpu.VMEM((SAMPLE_SLOTS, n_pages, QK_ROPE, page), F32),
                        pltpu.SemaphoreType.DMA((2, SAMPLE_SLOTS))])
    return pl.pallas_call(
        functools.partial(_sattn_kernel, layer, n_pages, chunk_pages, t),
        grid_spec=grid_spec,
        out_shape=jax.ShapeDtypeStruct((dec_batch * t, MLA_HEADS * V_HEAD), F32),
        compiler_params=pltpu.CompilerParams(dimension_semantics=("arbitrary",),
                                             vmem_limit_bytes=VMEM_LIMIT),
        name="mla_sample",
    )(page_table, q, qabs, cnew, krnew, wkt, wv, cache_ckv, cache_krope_t)


def _out_kernel(x_ref, om_ref, y_ref, srg_ref, sga_ref, sgb_ref, gret_ref, wo_ref, ln2_ref,
                wup_ref, wdn_ref, o_ref):
    o_ret = y_ref[...] * gret_ref[...] * srg_ref[...].astype(F32)
    mixed = (sga_ref[...].astype(F32) * om_ref[...].astype(F32)
             + sgb_ref[...].astype(F32) * o_ret)
    x1 = x_ref[...] + _dot(mixed.astype(BF16), wo_ref[...])
    h2 = _rmsnorm(x1, ln2_ref[...]).astype(BF16)
    u = jnp.maximum(_dot(h2, wup_ref[...]), 0.0)
    o_ref[...] = x1 + _dot((u * u).astype(BF16), wdn_ref[...])


def _out_proj(layer, x, om, y, rg, ga, gb, w, tm):
    T = x.shape[0]
    lspec = functools.partial(_layer_spec, layer)
    row = pl.BlockSpec((tm, D_MODEL), lambda i: (i, 0))
    return pl.pallas_call(
        _out_kernel,
        grid=(T // tm,),
        in_specs=[row, row, row, row, row, row, lspec((1, D_MODEL)),
                  lspec((D_MODEL, D_MODEL)), lspec((1, D_MODEL)),
                  lspec((D_MODEL, D_FF)), lspec((D_FF, D_MODEL))],
        out_specs=row,
        out_shape=jax.ShapeDtypeStruct((T, D_MODEL), F32),
        compiler_params=pltpu.CompilerParams(dimension_semantics=("parallel",),
                                             vmem_limit_bytes=VMEM_LIMIT),
        name="out_proj",
    )(x, om, y, rg, ga, gb, w["g_ret"], w["w_o"], w["ln2"], w["w_up"], w["w_down"])


def _prep_weights(ln1, w_in, g_qlat, w_uq, g_qn, g_qr, g_kvlat, g_kr, w_ukv, g_kn, g_ret,
                  w_o, ln2, w_up, w_down):
    depth = w_in.shape[0]
    lat = Q_LORA + KV_LORA
    qk = QK_NOPE + QK_ROPE
    zeros = lambda *shape: jnp.zeros((depth,) + shape, F32)
    w_in_p = jnp.concatenate(
        [w_in[..., :lat], w_in[..., lat + QK_ROPE:], zeros(D_MODEL, ROPE_LO),
         w_in[..., lat:lat + QK_ROPE], zeros(D_MODEL, LANES - ROPE_LO - QK_ROPE)], axis=-1)
    w_uq_p = jnp.pad(w_uq.reshape(depth, Q_LORA, MLA_HEADS, qk),
                     ((0, 0), (0, 0), (0, 0), (0, LANES - qk)))
    kv = w_ukv.reshape(depth, KV_LORA, MLA_HEADS, QK_NOPE + V_HEAD)
    wk, wv = kv[..., :QK_NOPE], kv[..., QK_NOPE:]
    w_uk_p = jnp.pad(wk, ((0, 0), (0, 0), (0, 0), (0, LANES - QK_NOPE)))
    wkt = wk.transpose(0, 2, 3, 1)
    scale = float(qk ** -0.5 * np.log2(np.e))
    vec = lambda v: v.reshape(depth, 1, -1)
    return {
        "ln1": vec(ln1), "w_in": w_in_p.astype(BF16), "g_qlat": vec(g_qlat),
        "w_uq": w_uq_p.reshape(depth, Q_LORA, MLA_HEADS * LANES).astype(BF16),
        "g_q": vec(jnp.concatenate([g_qn, g_qr, zeros(LANES - qk)], axis=-1) * scale),
        "g_kvlat": vec(g_kvlat),
        "g_kr": vec(jnp.concatenate([zeros(ROPE_LO), g_kr, zeros(LANES - ROPE_LO - QK_ROPE)],
                                    axis=-1)),
        "g_k": vec(jnp.concatenate([g_kn, zeros(LANES - QK_NOPE)], axis=-1)),
        "w_uk": w_uk_p.reshape(depth, KV_LORA, MLA_HEADS * LANES).astype(BF16),
        "w_uvt": wv.transpose(0, 2, 3, 1).reshape(depth, MLA_HEADS * V_HEAD, KV_LORA).astype(BF16),
        "w_ukt": jnp.pad(wkt, ((0, 0), (0, 0), (0, LANES - QK_NOPE), (0, 0))).reshape(
            depth, MLA_HEADS * LANES, KV_LORA).astype(BF16),
        "w_ukt_all": wkt.reshape(depth, MLA_HEADS * QK_NOPE, KV_LORA).astype(BF16),
        "w_uv": wv.reshape(depth, KV_LORA, MLA_HEADS * V_HEAD).astype(BF16),
        "gmat": jnp.asarray(_group_mean_matrix(), BF16),
        "g_ret": vec(g_ret), "w_o": w_o.astype(BF16), "ln2": vec(ln2),
        "w_up": w_up.astype(BF16), "w_down": w_down.astype(BF16),
    }


def _rope_tables(pos):
    def cos_sin(dim):
        inv = ROPE_THETA ** (-jnp.arange(0, dim, 2, dtype=F32) / dim)
        ang = pos.astype(F32)[:, None] * inv[None, :]
        return jnp.cos(ang), jnp.sin(ang)
    n = pos.shape[0]
    c_r, s_r = cos_sin(RET_QK)
    c_m, s_m = cos_sin(QK_ROPE)
    tail = jnp.zeros((n, LANES - ROPE_LO - QK_ROPE), F32)
    return {
        "cr": jnp.concatenate([c_r, c_r], axis=1),
        "sr": jnp.concatenate([-s_r, s_r], axis=1),
        "cm": jnp.concatenate([jnp.ones((n, ROPE_LO), F32), c_m, c_m, tail], axis=1),
        "sm": jnp.concatenate([jnp.zeros((n, ROPE_LO), F32), -s_m, s_m, tail], axis=1),
    }


def kernel(x_prompt, x_sample, cache_ckv, cache_krope, state_ret, page_table, ln1, w_in, g_qlat,
           w_uq, g_qn, g_qr, g_kvlat, g_kr, w_ukv, g_kn, g_ret, w_o, ln2, w_up, w_down):
    batch, seq, _ = x_prompt.shape
    dec_batch, dec_seq, _ = x_sample.shape
    depth = w_in.shape[0]
    n_pages = page_table.shape[1]
    past = n_pages * cache_ckv.shape[2]

    tm_p = min(256, seq)
    tm_out = min(512, seq)
    tm_s = min(256, dec_batch * dec_seq)
    tq = min(512, seq)
    ret_chunk = min(256, seq)
    ret_ts = min(1024, seq)
    chunk_pages = min(16, n_pages)
    bb = min(8, dec_batch)
    cache_krope_t = jnp.swapaxes(cache_krope, 2, 3)
    st_sample = jnp.zeros(state_ret.shape, F32)

    tabs_p = _rope_tables(jnp.arange(seq))
    tabs_s = {k: jnp.tile(v, (tm_s // dec_seq, 1))
              for k, v in _rope_tables(past + jnp.arange(dec_seq)).items()}
    lgtab = jnp.broadcast_to(
        jnp.log1p(-jnp.exp2(-5.0 - jnp.arange(RET_HEADS, dtype=F32)))[:, None, None],
        (RET_HEADS, 8, LANES))

    xp = x_prompt.reshape(batch * seq, D_MODEL)
    xs = x_sample.reshape(dec_batch * dec_seq, D_MODEL)
    outs = [[] for _ in range(5)]
    w = _prep_weights(ln1, w_in, g_qlat, w_uq, g_qn, g_qr, g_kvlat, g_kr, w_ukv, g_kn, g_ret,
                      w_o, ln2, w_up, w_down)
    for l in range(depth):
        q, ckv, kr, rq, rk, rv, rg, ga, gb, k, vt = _in_proj(
            "prompt", l, xp, w, tabs_p, seq // tm_p, tm_p, batch, seq)
        o_mla = _mla_prompt(q, k, vt, batch, seq, tq, 4)
        y, st = _ret_prompt(lgtab, rq, rk, rv, batch, seq, ret_chunk, ret_ts)
        xp = _out_proj(l, xp, o_mla, y, rg, ga, gb, w, tm_out)
        outs[0].append(ckv.reshape(batch, seq, KV_LORA))
        outs[1].append(kr.reshape(batch, seq, QK_ROPE))
        outs[2].append(st)

        q, ckv, kr, rq, rk, rv, rg, ga, gb, qabs = _in_proj(
            "sample", l, xs, w, tabs_s, 1, tm_s, dec_batch, dec_seq)
        o_mla = _mla_sample(l, page_table, q, qabs, ckv, kr, w["w_ukt_all"], w["w_uv"],
                            cache_ckv, cache_krope_t, chunk_pages)
        y, st_sample = _ret_sample(l, lgtab, rq, rk, rv, state_ret, st_sample, bb)
        xs = _out_proj(l, xs, o_mla, y, rg, ga, gb, w, tm_s)
        outs[3].append(ckv.reshape(dec_batch, dec_seq, KV_LORA))
        outs[4].append(kr.reshape(dec_batch, dec_seq, QK_ROPE))

    return (xp.reshape(batch, seq, D_MODEL), xs.reshape(dec_batch, dec_seq, D_MODEL),
            jnp.stack(outs[0]), jnp.stack(outs[1]), jnp.stack(outs[2]),
            jnp.stack(outs[3]), jnp.stack(outs[4]), st_sample)
```
